```python
import jax, jax.numpy as jnp
from jax import lax
import numpy as np

D_MODEL = 1024
BATCH = 4
SEQ = 8192
DEPTH = 4

CHUNK = 64
Q_BLOCK = 128
N_MIXERS = 2
EPS = 1e-6

MLA_HEADS = 16
MLA_NOPE = 64
MLA_ROPE = 32
MLA_V = 64
MLA_Q_LORA = 384
MLA_KV_LORA = 256
MLA_IN = MLA_Q_LORA + MLA_KV_LORA + MLA_ROPE
ROPE_THETA = 10000.0

SSM_EXPAND = 2
D_INNER = SSM_EXPAND * D_MODEL
SSM_HEADDIM = 64
SSM_HEADS = D_INNER // SSM_HEADDIM
SSM_GROUPS = 4
SSM_HPG = SSM_HEADS // SSM_GROUPS
SSM_STATE = 128
SSM_CONV = 4
SSM_CHUNK = CHUNK
CONV_DIM = D_INNER + 2 * SSM_GROUPS * SSM_STATE
SSM_IN = D_INNER + CONV_DIM + SSM_HEADS

D_FF = 2816
N_EXPERTS = 8
TOP_K = 2
D_EXPERT = 2816

kernel_name = "hybrid_mla_mamba2_moe_adaln_stream"


def rmsnorm(x, g):
    xf = x.astype(jnp.float32)
    y = xf * lax.rsqrt(jnp.mean(xf * xf, axis=-1, keepdims=True) + EPS)
    return (y * g.astype(jnp.float32)).astype(x.dtype)


def modulate(h, shift, scale):
    return h * (1.0 + scale[:, None, :]) + shift[:, None, :]


def rope_tables(positions):
    inv_freq = ROPE_THETA ** (-jnp.arange(0, MLA_ROPE, 2, dtype=jnp.float32) / MLA_ROPE)
    ang = positions.astype(jnp.float32)[..., None] * inv_freq
    return jnp.cos(ang), jnp.sin(ang)


def apply_rope(t, cos, sin):
    t1, t2 = jnp.split(t, 2, axis=-1)
    cos = cos.astype(t.dtype)
    sin = sin.astype(t.dtype)
    return jnp.concatenate([t1 * cos - t2 * sin, t1 * sin + t2 * cos], axis=-1)


def chunk_causal_mla_attention(q_nope, q_rope, k_nope, k_rope, v):
    bsz, seq, nh, _ = q_nope.shape
    nb = seq // Q_BLOCK
    scale = (MLA_NOPE + MLA_ROPE) ** -0.5
    qn = q_nope.reshape(bsz, nb, Q_BLOCK, nh, MLA_NOPE).transpose(1, 0, 2, 3, 4)
    qr = q_rope.reshape(bsz, nb, Q_BLOCK, nh, MLA_ROPE).transpose(1, 0, 2, 3, 4)
    key_chunk = jnp.arange(seq) // CHUNK

    def one_block(args):
        blk, qn_b, qr_b = args
        s = (jnp.einsum('bqhd,bkhd->bhqk', qn_b, k_nope, preferred_element_type=jnp.float32)
             + jnp.einsum('bqhr,bkr->bhqk', qr_b, k_rope, preferred_element_type=jnp.float32)) * scale
        q_chunk = (blk * Q_BLOCK + jnp.arange(Q_BLOCK)) // CHUNK
        mask = key_chunk[None, :] <= q_chunk[:, None]
        s = jnp.where(mask[None, None], s, -jnp.inf)
        p = jax.nn.softmax(s, axis=-1).astype(v.dtype)
        return jnp.einsum('bhqk,bkhd->bqhd', p, v)

    o = lax.map(one_block, (jnp.arange(nb), qn, qr))
    return o.transpose(1, 0, 2, 3, 4).reshape(bsz, seq, nh, MLA_V)


def mla_mixer(h, cos, sin, w_in, q_norm, kv_norm, w_uq, w_ukv, w_out):
    bsz, seq, _ = h.shape
    proj = h @ w_in
    c_q = rmsnorm(proj[..., :MLA_Q_LORA], q_norm)
    c_kv = rmsnorm(proj[..., MLA_Q_LORA:MLA_Q_LORA + MLA_KV_LORA], kv_norm)
    k_rope = apply_rope(proj[..., MLA_Q_LORA + MLA_KV_LORA:], cos, sin)
    q = (c_q @ w_uq).reshape(bsz, seq, MLA_HEADS, MLA_NOPE + MLA_ROPE)
    q_nope = q[..., :MLA_NOPE]
    q_rope = apply_rope(q[..., MLA_NOPE:], cos[:, :, None, :], sin[:, :, None, :])
    kv = (c_kv @ w_ukv).reshape(bsz, seq, MLA_HEADS, MLA_NOPE + MLA_V)
    k_nope = kv[..., :MLA_NOPE]
    v = kv[..., MLA_NOPE:]
    o = chunk_causal_mla_attention(q_nope, q_rope, k_nope, k_rope, v)
    return o.reshape(bsz, seq, MLA_HEADS * MLA_V) @ w_out


def causal_depthwise_conv(u, w, b):
    y = lax.conv_general_dilated(
        u, w[:, None, :].astype(u.dtype), window_strides=(1,), padding=[(SSM_CONV - 1, 0)],
        dimension_numbers=('NWC', 'WIO', 'NWC'), feature_group_count=u.shape[-1])
    return y + b.astype(u.dtype)


def ssd_chunked(x, dt, a, bm, cm):
    bsz, seq = x.shape[:2]
    nc = seq // SSM_CHUNK
    L = SSM_CHUNK
    x = x.reshape(bsz, nc, L, SSM_GROUPS, SSM_HPG, SSM_HEADDIM)
    dt = dt.reshape(bsz, nc, L, SSM_GROUPS, SSM_HPG)
    bm = bm.reshape(bsz, nc, L, SSM_GROUPS, SSM_STATE)
    cm = cm.reshape(bsz, nc, L, SSM_GROUPS, SSM_STATE)
    a_cs = jnp.cumsum(dt * a.reshape(SSM_GROUPS, SSM_HPG), axis=2)
    xdt = x * dt[..., None]
    a_t = a_cs.transpose(0, 1, 3, 4, 2)
    seg = a_t[..., :, None] - a_t[..., None, :]
    causal = jnp.tril(jnp.ones((L, L), dtype=bool))
    decay = jnp.where(causal, jnp.exp(jnp.where(causal, seg, 0.0)), 0.0)
    cb = jnp.einsum('bclgn,bcsgn->bcgls', cm, bm)
    y_diag = jnp.einsum('bcgls,bcghls,bcsghp->bclghp', cb, decay, xdt)
    decay_to_end = jnp.exp(a_cs[:, :, -1:] - a_cs)
    states = jnp.einsum('bclgn,bclgh,bclghp->bcghpn', bm, decay_to_end, xdt)
    chunk_decay = jnp.exp(a_cs[:, :, -1])

    def step(state, inp):
        st_c, dec_c = inp
        return state * dec_c[..., None, None] + st_c, state

    init = jnp.zeros((bsz, SSM_GROUPS, SSM_HPG, SSM_HEADDIM, SSM_STATE), jnp.float32)
    _, prev = lax.scan(step, init, (jnp.moveaxis(states, 1, 0), jnp.moveaxis(chunk_decay, 1, 0)))
    prev = jnp.moveaxis(prev, 0, 1)
    y_off = jnp.einsum('bclgn,bcghpn,bclgh->bclghp', cm, prev, jnp.exp(a_cs))
    return (y_diag + y_off).reshape(bsz, seq, SSM_HEADS, SSM_HEADDIM)


def mamba2_mixer(h, w_in, conv_w, conv_b, dt_bias, a_log, d_skip, norm_g, w_out):
    bsz, seq, _ = h.shape
    proj = h @ w_in
    z = proj[..., :D_INNER]
    xbc = jax.nn.silu(causal_depthwise_conv(proj[..., D_INNER:D_INNER + CONV_DIM], conv_w, conv_b))
    dt_raw = proj[..., D_INNER + CONV_DIM:]
    xs = xbc[..., :D_INNER].reshape(bsz, seq, SSM_HEADS, SSM_HEADDIM).astype(jnp.float32)
    bm = xbc[..., D_INNER:D_INNER + SSM_GROUPS * SSM_STATE].reshape(bsz, seq, SSM_GROUPS, SSM_STATE).astype(jnp.float32)
    cm = xbc[..., D_INNER + SSM_GROUPS * SSM_STATE:].reshape(bsz, seq, SSM_GROUPS, SSM_STATE).astype(jnp.float32)
    dt = jax.nn.softplus(dt_raw.astype(jnp.float32) + dt_bias.astype(jnp.float32))
    a = -jnp.exp(a_log.astype(jnp.float32))
    y = ssd_chunked(xs, dt, a, bm, cm) + d_skip.astype(jnp.float32)[:, None] * xs
    y = y.reshape(bsz, seq, D_INNER).astype(h.dtype)
    yg = (y * jax.nn.silu(z)).reshape(bsz, seq, SSM_GROUPS, D_INNER // SSM_GROUPS)
    yf = yg.astype(jnp.float32)
    yf = yf * lax.rsqrt(jnp.mean(yf * yf, axis=-1, keepdims=True) + EPS)
    y = (yf.reshape(bsz, seq, D_INNER) * norm_g.astype(jnp.float32)).astype(h.dtype)
    return y @ w_out


def swiglu(t, w_gate, w_up, w_down):
    return (jax.nn.silu(t @ w_gate) * (t @ w_up)) @ w_down


def moe_swiglu(h, w_router, w_gate, w_up, w_down):
    bsz, seq, d = h.shape
    t = h.reshape(bsz * seq, d)
    logits = (t @ w_router).astype(jnp.float32)
    top_vals, top_idx = lax.top_k(logits, TOP_K)
    top_w = jax.nn.softmax(top_vals, axis=-1)
    gates = jnp.sum(jax.nn.one_hot(top_idx, N_EXPERTS, dtype=jnp.float32) * top_w[..., None], axis=1)
    out = jnp.zeros_like(t)
    for e in range(N_EXPERTS):
        out = out + gates[:, e:e + 1].astype(t.dtype) * swiglu(t, w_gate[e], w_up[e], w_down[e])
    return out.reshape(bsz, seq, d)


def setup_inputs(seed: int = 0) -> dict:
    key = jax.random.key(seed)
    ks = jax.random.split(key, 40)
    n_att = (DEPTH + 1) // 2
    n_ssm = DEPTH // 2
    f32 = jnp.float32

    def nrm(i, shape, fan_in, mult=1.0):
        return jax.random.normal(ks[i], shape, f32) * (mult * fan_in ** -0.5)

    def gain(i, shape):
        return 1.0 + 0.1 * jax.random.normal(ks[i], shape, f32)

    x = jax.random.normal(ks[0], (BATCH, SEQ, D_MODEL), f32)
    c = jax.random.normal(ks[1], (BATCH, D_MODEL), f32)
    offsets = jax.random.randint(ks[2], (BATCH, 1), 0, 4096, dtype=jnp.int32)
    positions = (jnp.arange(SEQ, dtype=jnp.int32)[None, :] + offsets).astype(jnp.int32)
    dt0 = jnp.exp(jax.random.uniform(ks[3], (n_ssm, SSM_HEADS), f32, np.log(1e-3), np.log(1e-1)))
    return {
        "x": x,
        "c": c,
        "positions": positions,
        "ada_w": nrm(4, (DEPTH, D_MODEL, 6 * D_MODEL), D_MODEL, 0.5),
        "ada_b": 0.02 * jax.random.normal(ks[5], (DEPTH, 6 * D_MODEL), f32),
        "norm_g": gain(6, (DEPTH, 2, D_MODEL)),
        "mla_w_in": nrm(7, (n_att, D_MODEL, MLA_IN), D_MODEL),
        "mla_q_norm": gain(8, (n_att, MLA_Q_LORA)),
        "mla_kv_norm": gain(9, (n_att, MLA_KV_LORA)),
        "mla_w_uq": nrm(10, (n_att, MLA_Q_LORA, MLA_HEADS * (MLA_NOPE + MLA_ROPE)), MLA_Q_LORA),
        "mla_w_ukv": nrm(11, (n_att, MLA_KV_LORA, MLA_HEADS * (MLA_NOPE + MLA_V)), MLA_KV_LORA),
        "mla_w_out": nrm(12, (n_att, MLA_HEADS * MLA_V, D_MODEL), MLA_HEADS * MLA_V),
        "ssm_w_in": nrm(13, (n_ssm, D_MODEL, SSM_IN), D_MODEL),
        "ssm_conv_w": nrm(14, (n_ssm, SSM_CONV, CONV_DIM), SSM_CONV),
        "ssm_conv_b": 0.02 * jax.random.normal(ks[15], (n_ssm, CONV_DIM), f32),
        "ssm_dt_bias": dt0 + jnp.log(-jnp.expm1(-dt0)),
        "ssm_a_log": jnp.log(jax.random.uniform(ks[16], (n_ssm, SSM_HEADS), f32, 1.0, 16.0)),
        "ssm_d": gain(17, (n_ssm, SSM_HEADS)),
        "ssm_norm": gain(18, (n_ssm, D_INNER)),
        "ssm_w_out": nrm(19, (n_ssm, D_INNER, D_MODEL), D_INNER),
        "ffn_w_gate": nrm(20, (n_att, D_MODEL, D_FF), D_MODEL),
        "ffn_w_up": nrm(21, (n_att, D_MODEL, D_FF), D_MODEL),
        "ffn_w_down": nrm(22, (n_att, D_FF, D_MODEL), D_FF),
        "moe_w_router": nrm(23, (n_ssm, D_MODEL, N_EXPERTS), D_MODEL),
        "moe_w_gate": nrm(24, (n_ssm, N_EXPERTS, D_MODEL, D_EXPERT), D_MODEL),
        "moe_w_up": nrm(25, (n_ssm, N_EXPERTS, D_MODEL, D_EXPERT), D_MODEL),
        "moe_w_down": nrm(26, (n_ssm, N_EXPERTS, D_EXPERT, D_MODEL), D_EXPERT),
        "final_norm": gain(27, (D_MODEL,)),
    }


def reference(x, c, positions, ada_w, ada_b, norm_g,
              mla_w_in, mla_q_norm, mla_kv_norm, mla_w_uq, mla_w_ukv, mla_w_out,
              ssm_w_in, ssm_conv_w, ssm_conv_b, ssm_dt_bias, ssm_a_log, ssm_d, ssm_norm, ssm_w_out,
              ffn_w_gate, ffn_w_up, ffn_w_down,
              moe_w_router, moe_w_gate, moe_w_up, moe_w_down, final_norm):
    cos, sin = rope_tables(positions)
    cond = jax.nn.silu(c)
    for i in range(DEPTH):
        j = i // N_MIXERS
        mod = cond @ ada_w[i] + ada_b[i]
        sh1, sc1, g1, sh2, sc2, g2 = jnp.split(mod, 6, axis=-1)
        h = modulate(rmsnorm(x, norm_g[i, 0]), sh1, sc1)
        if i % N_MIXERS == 0:
            y = mla_mixer(h, cos, sin, mla_w_in[j], mla_q_norm[j], mla_kv_norm[j],
                          mla_w_uq[j], mla_w_ukv[j], mla_w_out[j])
        else:
            y = mamba2_mixer(h, ssm_w_in[j], ssm_conv_w[j], ssm_conv_b[j], ssm_dt_bias[j],
                             ssm_a_log[j], ssm_d[j], ssm_norm[j], ssm_w_out[j])
        x = x + g1[:, None, :] * y
        h = modulate(rmsnorm(x, norm_g[i, 1]), sh2, sc2)
        if i % 2 == 0:
            y = swiglu(h, ffn_w_gate[j], ffn_w_up[j], ffn_w_down[j])
        else:
            y = moe_swiglu(h, moe_w_router[j], moe_w_gate[j], moe_w_up[j], moe_w_down[j])
        x = x + g2[:, None, :] * y
    return rmsnorm(x, final_norm)
```

```python
import functools

import jax
import jax.numpy as jnp
from jax import lax
from jax.experimental import pallas as pl
from jax.experimental.pallas import tpu as pltpu

F32 = jnp.float32
BF16 = jnp.bfloat16

EPS = 1e-6
CHUNK = 64
MLA_HEADS = 16
MLA_NOPE = 64
MLA_ROPE = 32
MLA_V = 64
MLA_Q_LORA = 384
MLA_KV_LORA = 256
ROPE_THETA = 10000.0
SSM_HEADDIM = 64
SSM_GROUPS = 4
SSM_STATE = 128
SSM_CONV = 4
N_EXPERTS = 8

LANES = 128
SUBLANES = 8
VMEM_LIMIT = 56 * 1024 * 1024

ROW_TILE = 512
FFN_ROW_TILE = 1024
FFN_COL_TILE = 256
ATTN_TILE = 512
SSD_BLOCK = 512
ADA_COL_TILE = 1536

HEAD_PAD = LANES
ROPE_OFF = MLA_NOPE
NEG_BIG = -1e30


def _cparams(sem):
    return pltpu.CompilerParams(dimension_semantics=sem, vmem_limit_bytes=VMEM_LIMIT)


def _rmsnorm_mod(x, g, sh, sc):
    y = x * lax.rsqrt(jnp.mean(x * x, axis=-1, keepdims=True) + EPS) * g
    return y * (1.0 + sc) + sh


def _sigmoid(v):
    return 1.0 / (1.0 + jnp.exp(-v))


def _split3(v):
    hi = v.astype(BF16)
    r1 = v - hi.astype(F32)
    mid = r1.astype(BF16)
    lo = (r1 - mid.astype(F32)).astype(BF16)
    return hi, mid, lo


def _dot(a, b):
    return jnp.dot(a, b, preferred_element_type=F32)


def _dot_nt(a, b):
    return lax.dot_general(a, b, (((1,), (1,)), ((), ())), preferred_element_type=F32)


def _adaln_kernel(c_ref, w_ref, b_ref, o_ref):
    c = c_ref[...]
    cond = (c * _sigmoid(c)).astype(BF16)
    o_ref[0] = _dot(cond, w_ref[0].astype(BF16)) + b_ref[0]


def _adaln(c_pad, ada_w, ada_b):
    depth, d, n = ada_w.shape
    bp = c_pad.shape[0]
    tn = min(ADA_COL_TILE, n)
    return pl.pallas_call(
        _adaln_kernel,
        grid=(depth, n // tn),
        in_specs=[
            pl.BlockSpec((bp, d), lambda l, j: (0, 0)),
            pl.BlockSpec((1, d, tn), lambda l, j: (l, 0, j)),
            pl.BlockSpec((1, 1, tn), lambda l, j: (l, 0, j)),
        ],
        out_specs=pl.BlockSpec((1, bp, tn), lambda l, j: (l, 0, j)),
        out_shape=jax.ShapeDtypeStruct((depth, bp, n), F32),
        compiler_params=_cparams(("parallel", "parallel")),
        name="adaln",
    )(c_pad, ada_w, ada_b.reshape(depth, 1, n))


def _mla_in_kernel(x_ref, g_ref, sh_ref, sc_ref, win_ref, qn_ref, kvn_ref, wuq_ref, wukv_ref,
                   cq_ref, sq_ref, ck_ref, sk_ref, q_out, k_out, v_out):
    h = _rmsnorm_mod(x_ref[...], g_ref[...], sh_ref[0], sc_ref[0]).astype(BF16)
    proj = _dot(h, win_ref[...])
    cq = proj[:, :MLA_Q_LORA]
    cq = (cq * lax.rsqrt(jnp.mean(cq * cq, axis=-1, keepdims=True) + EPS) * qn_ref[...]).astype(BF16)
    ckv = proj[:, MLA_Q_LORA:MLA_Q_LORA + MLA_KV_LORA]
    ckv = (ckv * lax.rsqrt(jnp.mean(ckv * ckv, axis=-1, keepdims=True) + EPS) * kvn_ref[...]).astype(BF16)
    r0 = MLA_Q_LORA + MLA_KV_LORA
    kr = proj[:, r0:r0 + HEAD_PAD] * ck_ref[...] + proj[:, r0 + HEAD_PAD:r0 + 2 * HEAD_PAD] * sk_ref[...]
    nq = MLA_HEADS * HEAD_PAD
    qa = _dot(cq, wuq_ref[:, :nq])
    qb = _dot(cq, wuq_ref[:, nq:])
    kv = _dot(ckv, wukv_ref[...])
    cqt = cq_ref[...]
    sqt = sq_ref[...]
    for hd in range(MLA_HEADS):
        sl = slice(hd * HEAD_PAD, (hd + 1) * HEAD_PAD)
        q_out[:, sl] = (qa[:, sl] * cqt + qb[:, sl] * sqt).astype(BF16)
        k_out[:, sl] = (kv[:, sl] + kr).astype(BF16)
    v_out[...] = kv[:, nq:].astype(BF16)


def _mla_in(x2, g, sh, sc, win, qn, kvn, wuq, wukv, cq, sq, ck, sk, seq):
    t, d = x2.shape
    tm = min(ROW_TILE, seq)
    tpb = seq // tm
    nq = MLA_HEADS * HEAD_PAD
    nv = MLA_HEADS * MLA_V
    row = lambda i: (i, 0)
    fixed = lambda i: (0, 0)
    per_b = lambda i: (i // tpb, 0, 0)
    return pl.pallas_call(
        _mla_in_kernel,
        grid=(t // tm,),
        in_specs=[
            pl.BlockSpec((tm, d), row),
            pl.BlockSpec((1, d), fixed),
            pl.BlockSpec((1, 1, d), per_b),
            pl.BlockSpec((1, 1, d), per_b),
            pl.BlockSpec(win.shape, fixed),
            pl.BlockSpec(qn.shape, fixed),
            pl.BlockSpec(kvn.shape, fixed),
            pl.BlockSpec(wuq.shape, fixed),
            pl.BlockSpec(wukv.shape, fixed),
            pl.BlockSpec((tm, HEAD_PAD), row),
            pl.BlockSpec((tm, HEAD_PAD), row),
            pl.BlockSpec((tm, HEAD_PAD), row),
            pl.BlockSpec((tm, HEAD_PAD), row),
        ],
        out_specs=[
            pl.BlockSpec((tm, nq), row),
            pl.BlockSpec((tm, nq), row),
            pl.BlockSpec((tm, nv), row),
        ],
        out_shape=[
            jax.ShapeDtypeStruct((t, nq), BF16),
            jax.ShapeDtypeStruct((t, nq), BF16),
            jax.ShapeDtypeStruct((t, nv), BF16),
        ],
        compiler_params=_cparams(("parallel",)),
        name="mla_in",
    )(x2, g, sh, sc, win, qn, kvn, wuq, wukv, cq, sq, ck, sk)


def _attn_kernel(qi_tab, kj_tab, q_ref, k_ref, v_ref, o_ref, m_sc, l_sc, acc_sc):
    t = pl.program_id(2)
    qi = qi_tab[t]
    kj = kj_tab[t]
    tq = q_ref.shape[1]
    tk = k_ref.shape[1]
    left = lax.broadcasted_iota(jnp.int32, (tq, LANES), 1) < MLA_V

    @pl.when(kj == 0)
    def _():
        m_sc[...] = jnp.full(m_sc.shape, NEG_BIG, F32)
        l_sc[...] = jnp.zeros(l_sc.shape, F32)
        acc_sc[...] = jnp.zeros(acc_sc.shape, F32)

    def step(masked):
        q = q_ref[0]
        k = k_ref[0]
        v = v_ref[0]
        alphas = []
        pvs = []
        for hh in range(2):
            sl = slice(hh * HEAD_PAD, (hh + 1) * HEAD_PAD)
            s = _dot_nt(q[:, sl], k[:, sl])
            if masked:
                rc = lax.broadcasted_iota(jnp.int32, (tq, tk), 0) // CHUNK
                cc = lax.broadcasted_iota(jnp.int32, (tq, tk), 1) // CHUNK
                s = jnp.where(cc <= rc, s, NEG_BIG)
            m_prev = m_sc[hh]
            m_new = jnp.maximum(m_prev, jnp.max(s, axis=-1, keepdims=True))
            alpha = jnp.exp(m_prev - m_new)
            p = jnp.exp(s - m_new[:, :1])
            l_sc[hh] = alpha * l_sc[hh] + jnp.sum(p, axis=-1, keepdims=True)
            m_sc[hh] = m_new
            alphas.append(alpha)
            pvs.append(_dot(p.astype(BF16), v))
        acc_sc[...] = acc_sc[...] * jnp.where(left, alphas[0], alphas[1]) + jnp.where(left, pvs[0], pvs[1])

    @pl.when(kj < qi)
    def _():
        step(False)

    @pl.when(kj == qi)
    def _():
        step(True)
        o_ref[0] = (acc_sc[...] / jnp.where(left, l_sc[0], l_sc[1])).astype(o_ref.dtype)


def _attention(q, k, v, bsz, seq):
    tile = min(ATTN_TILE, seq)
    nq = seq // tile
    qi_list, kj_list = [], []
    for a in range(nq):
        for b in range(a + 1):
            qi_list.append(a)
            kj_list.append(b)
    qi_tab = jnp.asarray(qi_list, jnp.int32)
    kj_tab = jnp.asarray(kj_list, jnp.int32)
    q3 = q.reshape(bsz, seq, MLA_HEADS * HEAD_PAD)
    k3 = k.reshape(bsz, seq, MLA_HEADS * HEAD_PAD)
    v3 = v.reshape(bsz, seq, MLA_HEADS * MLA_V)
    grid_spec = pltpu.PrefetchScalarGridSpec(
        num_scalar_prefetch=2,
        grid=(bsz, MLA_HEADS // 2, len(qi_list)),
        in_specs=[
            pl.BlockSpec((1, tile, 2 * HEAD_PAD), lambda b, hp, t, qt, kt: (b, qt[t], hp)),
            pl.BlockSpec((1, tile, 2 * HEAD_PAD), lambda b, hp, t, qt, kt: (b, kt[t], hp)),
            pl.BlockSpec((1, tile, 2 * MLA_V), lambda b, hp, t, qt, kt: (b, kt[t], hp)),
        ],
        out_specs=pl.BlockSpec((1, tile, 2 * MLA_V), lambda b, hp, t, qt, kt: (b, qt[t], hp)),
        scratch_shapes=[
            pltpu.VMEM((2, tile, LANES), F32),
            pltpu.VMEM((2, tile, LANES), F32),
            pltpu.VMEM((tile, LANES), F32),
        ],
    )
    o = pl.pallas_call(
        _attn_kernel,
        grid_spec=grid_spec,
        out_shape=jax.ShapeDtypeStruct((bsz, seq, MLA_HEADS * MLA_V), BF16),
        compiler_params=_cparams(("parallel", "parallel", "arbitrary")),
        name="mla_attention",
    )(qi_tab, kj_tab, q3, k3, v3)
    return o.reshape(bsz * seq, MLA_HEADS * MLA_V)


def _proj_res_kernel(a_ref, w_ref, x_ref, gate_ref, o_ref):
    o_ref[...] = x_ref[...] + gate_ref[0] * _dot(a_ref[...], w_ref[...])


def _proj_residual(a, w, x2, gate, seq):
    t, d = x2.shape
    kdim = a.shape[1]
    tm = min(ROW_TILE, seq)
    tpb = seq // tm
    return pl.pallas_call(
        _proj_res_kernel,
        grid=(t // tm,),
        in_specs=[
            pl.BlockSpec((tm, kdim), lambda i: (i, 0)),
            pl.BlockSpec((kdim, d), lambda i: (0, 0)),
            pl.BlockSpec((tm, d), lambda i: (i, 0)),
            pl.BlockSpec((1, 1, d), lambda i: (i // tpb, 0, 0)),
        ],
        out_specs=pl.BlockSpec((tm, d), lambda i: (i, 0)),
        out_shape=jax.ShapeDtypeStruct((t, d), F32),
        compiler_params=_cparams(("parallel",)),
        name="proj_residual",
    )(a, w, x2, gate)


def _ffn_kernel(x_ref, g_ref, sh_ref, sc_ref, gate_ref, rw_ref, wg_ref, wu_ref, wd_ref, o_ref,
                h_sc, acc_sc, rw_sc, *, use_router_weights):
    e = pl.program_id(1)
    f = pl.program_id(2)

    @pl.when((e == 0) & (f == 0))
    def _():
        h_sc[...] = _rmsnorm_mod(x_ref[...], g_ref[...], sh_ref[0], sc_ref[0]).astype(BF16)
        acc_sc[...] = jnp.zeros(acc_sc.shape, F32)

    if use_router_weights:
        @pl.when(f == 0)
        def _():
            rw = rw_ref[...]
            lane = lax.broadcasted_iota(jnp.int32, rw.shape, 1)
            col = jnp.sum(jnp.where(lane == e, rw, 0.0), axis=-1, keepdims=True)
            rw_sc[...] = jnp.broadcast_to(col, rw_sc.shape)

    h = h_sc[...]
    gt = _dot(h, wg_ref[0])
    up = _dot(h, wu_ref[0])
    a = gt * _sigmoid(gt) * up
    if use_router_weights:
        a = a * rw_sc[:, :1]
    acc_sc[...] += _dot(a.astype(BF16), wd_ref[0])

    @pl.when((e == pl.num_programs(1) - 1) & (f == pl.num_programs(2) - 1))
    def _():
        o_ref[...] = x_ref[...] + gate_ref[0] * acc_sc[...]


def _ffn(x2, g, sh, sc, gate, rw, wg, wu, wd, seq, use_router_weights):
    t, d = x2.shape
    ne, _, dff = wg.shape
    tm = min(FFN_ROW_TILE, seq)
    tf = min(FFN_COL_TILE, dff)
    tpb = seq // tm
    per_b = lambda i, e, f: (i // tpb, 0, 0)
    return pl.pallas_call(
        functools.partial(_ffn_kernel, use_router_weights=use_router_weights),
        grid=(t // tm, ne, dff // tf),
        in_specs=[
            pl.BlockSpec((tm, d), lambda i, e, f: (i, 0)),
            pl.BlockSpec((1, d), lambda i, e, f: (0, 0)),
            pl.BlockSpec((1, 1, d), per_b),
            pl.BlockSpec((1, 1, d), per_b),
            pl.BlockSpec((1, 1, d), per_b),
            pl.BlockSpec((tm, LANES), lambda i, e, f: (i, 0)),
            pl.BlockSpec((1, d, tf), lambda i, e, f: (e, 0, f)),
            pl.BlockSpec((1, d, tf), lambda i, e, f: (e, 0, f)),
            pl.BlockSpec((1, tf, d), lambda i, e, f: (e, f, 0)),
        ],
        out_specs=pl.BlockSpec((tm, d), lambda i, e, f: (i, 0)),
        out_shape=jax.ShapeDtypeStruct((t, d), F32),
        scratch_shapes=[
            pltpu.VMEM((tm, d), BF16),
            pltpu.VMEM((tm, d), F32),
            pltpu.VMEM((tm, LANES), F32),
        ],
        compiler_params=_cparams(("parallel", "arbitrary", "arbitrary")),
        name="swiglu",
    )(x2, g, sh, sc, gate, rw, wg, wu, wd)


def _router_kernel(x_ref, g_ref, sh_ref, sc_ref, whi_ref, wlo_ref, o_ref):
    h = _rmsnorm_mod(x_ref[...], g_ref[...], sh_ref[0], sc_ref[0])
    h_hi = h.astype(BF16)
    h_lo = (h - h_hi.astype(F32)).astype(BF16)
    whi = whi_ref[...]
    logits = _dot(h_hi, whi) + _dot(h_lo, whi) + _dot(h_hi, wlo_ref[...])
    lane = lax.broadcasted_iota(jnp.int32, logits.shape, 1)
    neg = jnp.float32(-jnp.inf)
    lg = jnp.where(lane < N_EXPERTS, logits, neg)
    m1 = jnp.max(lg, axis=-1, keepdims=True)
    i1 = jnp.min(jnp.where(lg == m1, lane, LANES), axis=-1, keepdims=True)
    lg2 = jnp.where(lane == i1, neg, lg)
    m2 = jnp.max(lg2, axis=-1, keepdims=True)
    i2 = jnp.min(jnp.where(lg2 == m2, lane, LANES), axis=-1, keepdims=True)
    e2 = jnp.exp(m2 - m1)
    w1 = 1.0 / (1.0 + e2)
    w2 = e2 / (1.0 + e2)
    o_ref[...] = jnp.where(lane == i1, w1, 0.0) + jnp.where(lane == i2, w2, 0.0)


def _router(x2, g, sh, sc, whi, wlo, seq):
    t, d = x2.shape
    tm = min(ROW_TILE, seq)
    tpb = seq // tm
    per_b = lambda i: (i // tpb, 0, 0)
    return pl.pallas_call(
        _router_kernel,
        grid=(t // tm,),
        in_specs=[
            pl.BlockSpec((tm, d), lambda i: (i, 0)),
            pl.BlockSpec((1, d), lambda i: (0, 0)),
            pl.BlockSpec((1, 1, d), per_b),
            pl.BlockSpec((1, 1, d), per_b),
            pl.BlockSpec((d, LANES), lambda i: (0, 0)),
            pl.BlockSpec((d, LANES), lambda i: (0, 0)),
        ],
        out_specs=pl.BlockSpec((tm, LANES), lambda i: (i, 0)),
        out_shape=jax.ShapeDtypeStruct((t, LANES), F32),
        compiler_params=_cparams(("parallel",)),
        name="moe_router",
    )(x2, g, sh, sc, whi, wlo)


def _ssm_in_kernel(x_ref, g_ref, sh_ref, sc_ref, wz_ref, wx_ref, wdt_ref, z_out, xbc_out, dt_out):
    h = _rmsnorm_mod(x_ref[...], g_ref[...], sh_ref[0], sc_ref[0]).astype(BF16)
    z_out[...] = _dot(h, wz_ref[...]).astype(BF16)
    xbc_out[...] = _dot(h, wx_ref[...]).astype(BF16)
    dt_out[...] = _dot(h, wdt_ref[...])


def _ssm_in(x2, g, sh, sc, wz, wx, wdt, seq):
    t, d = x2.shape
    tm = min(ROW_TILE, seq)
    tpb = seq // tm
    per_b = lambda i: (i // tpb, 0, 0)
    fixed = lambda i: (0, 0)
    row = lambda i: (i, 0)
    return pl.pallas_call(
        _ssm_in_kernel,
        grid=(t // tm,),
        in_specs=[
            pl.BlockSpec((tm, d), row),
            pl.BlockSpec((1, d), fixed),
            pl.BlockSpec((1, 1, d), per_b),
            pl.BlockSpec((1, 1, d), per_b),
            pl.BlockSpec(wz.shape, fixed),
            pl.BlockSpec(wx.shape, fixed),
            pl.BlockSpec(wdt.shape, fixed),
        ],
        out_specs=[
            pl.BlockSpec((tm, wz.shape[1]), row),
            pl.BlockSpec((tm, wx.shape[1]), row),
            pl.BlockSpec((tm, LANES), row),
        ],
        out_shape=[
            jax.ShapeDtypeStruct((t, wz.shape[1]), BF16),
            jax.ShapeDtypeStruct((t, wx.shape[1]), BF16),
            jax.ShapeDtypeStruct((t, LANES), F32),
        ],
        compiler_params=_cparams(("parallel",)),
        name="ssm_in",
    )(x2, g, sh, sc, wz, wx, wdt)


def _ssd_kernel(z_ref, xbc_ref, dt_ref, cw_ref, cb_ref, dtb_ref, alog_ref, dskip_ref, ng_ref,
                tri_ref, exp_ref, eye_ref, o_ref, ubuf, acs_sc, dtx_sc, state_sc, *, d_inner):
    blk = z_ref.shape[0]
    gw = d_inner // SSM_GROUPS
    gn = SSM_GROUPS * SSM_STATE
    pair = 2 * SSM_HEADDIM

    @pl.when(pl.program_id(1) == 0)
    def _():
        ubuf[0:SUBLANES, :] = jnp.zeros((SUBLANES, ubuf.shape[1]), F32)
        state_sc[...] = jnp.zeros(state_sc.shape, F32)

    ubuf[SUBLANES:SUBLANES + blk, :] = xbc_ref[...].astype(F32)

    lane = lax.broadcasted_iota(jnp.int32, (1, LANES), 1)
    a = jnp.where(lane < d_inner // SSM_HEADDIM, -jnp.exp(alog_ref[...]), 0.0)
    v = dt_ref[...] + dtb_ref[...]
    dt = jnp.maximum(v, 0.0) + jnp.log(1.0 + jnp.exp(-jnp.abs(v)))
    tri = tri_ref[...]
    acs = sum(_dot(tri, part) for part in _split3(dt * a))
    expand = exp_ref[...]
    acs_sc[...] = sum(_dot(part, expand) for part in _split3(acs))
    dtx_sc[...] = sum(_dot(part, expand) for part in _split3(dt))

    cw = cw_ref[...]
    cbias = cb_ref[...]
    dskip = dskip_ref[...]
    ngain = ng_ref[...]
    eye = eye_ref[...]
    row_i = lax.broadcasted_iota(jnp.int32, (CHUNK, pair), 0)
    col_i = lax.broadcasted_iota(jnp.int32, (CHUNK, pair), 1)
    col_s = jnp.where(col_i >= SSM_HEADDIM, col_i - SSM_HEADDIM, col_i)
    diag = row_i == col_s
    causal = col_s <= row_i
    first_head = col_i < SSM_HEADDIM

    def chunk(c, carry):
        r0 = pl.multiple_of(c * CHUNK, CHUNK)
        win = ubuf[pl.ds(r0, CHUNK + SUBLANES), :]
        u = cbias
        for j in range(SSM_CONV):
            lo = SUBLANES - (SSM_CONV - 1) + j
            u = u + cw[j:j + 1, :] * win[lo:lo + CHUNK, :]
        xbc = u * _sigmoid(u)
        xs = xbc[:, :d_inner]
        bm = xbc[:, d_inner:d_inner + gn].astype(BF16)
        cm = xbc[:, d_inner + gn:].astype(BF16)
        ae = acs_sc[pl.ds(r0, CHUNK), :]
        xdt = xs * dtx_sc[pl.ds(r0, CHUNK), :]
        last = ae[CHUNK - 1:CHUNK, :]
        from_start = jnp.exp(ae)
        xdte = (xdt * jnp.exp(last - ae)).astype(BF16)
        chunk_decay = jnp.exp(last)
        zc = z_ref[pl.ds(r0, CHUNK), :].astype(F32)
        zgate = zc * _sigmoid(zc)
        for g in range(SSM_GROUPS):
            bg = bm[:, g * SSM_STATE:(g + 1) * SSM_STATE]
            cg = cm[:, g * SSM_STATE:(g + 1) * SSM_STATE]
            gs = slice(g * gw, (g + 1) * gw)
            cbcb = _dot_nt(cg, jnp.concatenate([bg, bg], axis=0))
            s_prev = state_sc[g]
            y_off = _dot(cg, s_prev.astype(BF16)) * from_start[:, gs]
            bg_t = _dot_nt(eye, bg).astype(BF16)
            state_sc[g] = s_prev * chunk_decay[:, gs] + _dot(bg_t, xdte[:, gs])
            ys = []
            for pr in range(gw // pair):
                ps = slice(g * gw + pr * pair, g * gw + (pr + 1) * pair)
                dcol = ae[:, ps]
                drow = jnp.sum(jnp.where(diag, dcol, 0.0), axis=0, keepdims=True)
                decay = jnp.where(causal, jnp.exp(jnp.where(causal, dcol - drow, 0.0)), 0.0)
                m = (cbcb * decay).astype(BF16)
                xp = xdt[:, ps]
                bd = jnp.concatenate([jnp.where(first_head, xp, 0.0), jnp.where(first_head, 0.0, xp)],
                                     axis=0).astype(BF16)
                ys.append(_dot(m, bd) + y_off[:, pr * pair:(pr + 1) * pair])
            yg = (jnp.concatenate(ys, axis=1) + dskip[:, gs] * xs[:, gs]) * zgate[:, gs]
            yn = yg * lax.rsqrt(jnp.mean(yg * yg, axis=-1, keepdims=True) + EPS) * ngain[:, gs]
            o_ref[pl.ds(r0, CHUNK), gs] = yn.astype(o_ref.dtype)
        return carry

    lax.fori_loop(0, blk // CHUNK, chunk, 0)
    ubuf[0:SUBLANES, :] = ubuf[blk:blk + SUBLANES, :]


def _ssd(z, xbc, dt, cw, cb, dtb, alog, dskip, ng, bsz, seq):
    d_inner = z.shape[1]
    cdim = xbc.shape[1]
    blk = min(SSD_BLOCK, seq)
    nblk = seq // blk
    r = jnp.arange(blk)
    tri = (((r[:, None] // CHUNK) == (r[None, :] // CHUNK)) & (r[None, :] <= r[:, None])).astype(BF16)
    heads = d_inner // SSM_HEADDIM
    expand = (jnp.arange(LANES)[:, None] == (jnp.arange(d_inner)[None, :] // SSM_HEADDIM)).astype(BF16)
    eye = jnp.eye(SSM_STATE, dtype=BF16)
    del heads
    row = lambda b, i: (b * nblk + i, 0)
    fixed = lambda b, i: (0, 0)
    return pl.pallas_call(
        functools.partial(_ssd_kernel, d_inner=d_inner),
        grid=(bsz, nblk),
        in_specs=[
            pl.BlockSpec((blk, d_inner), row),
            pl.BlockSpec((blk, cdim), row),
            pl.BlockSpec((blk, LANES), row),
            pl.BlockSpec(cw.shape, fixed),
            pl.BlockSpec(cb.shape, fixed),
            pl.BlockSpec(dtb.shape, fixed),
            pl.BlockSpec(alog.shape, fixed),
            pl.BlockSpec(dskip.shape, fixed),
            pl.BlockSpec(ng.shape, fixed),
            pl.BlockSpec(tri.shape, fixed),
            pl.BlockSpec(expand.shape, fixed),
            pl.BlockSpec(eye.shape, fixed),
        ],
        out_specs=pl.BlockSpec((blk, d_inner), row),
        out_shape=jax.ShapeDtypeStruct((bsz * seq, d_inner), BF16),
        scratch_shapes=[
            pltpu.VMEM((blk + SUBLANES, cdim), F32),
            pltpu.VMEM((blk, d_inner), F32),
            pltpu.VMEM((blk, d_inner), F32),
            pltpu.VMEM((SSM_GROUPS, SSM_STATE, d_inner // SSM_GROUPS), F32),
        ],
        compiler_params=_cparams(("parallel", "arbitrary")),
        name="ssd",
    )(z, xbc, dt, cw, cb, dtb, alog, dskip, ng, tri, expand, eye)


def _final_norm_kernel(x_ref, g_ref, o_ref):
    x = x_ref[...]
    o_ref[...] = x * lax.rsqrt(jnp.mean(x * x, axis=-1, keepdims=True) + EPS) * g_ref[...]


def _final_norm(x2, g, seq):
    t, d = x2.shape
    tm = min(FFN_ROW_TILE, seq)
    return pl.pallas_call(
        _final_norm_kernel,
        grid=(t // tm,),
        in_specs=[pl.BlockSpec((tm, d), lambda i: (i, 0)), pl.BlockSpec((1, d), lambda i: (0, 0))],
        out_specs=pl.BlockSpec((tm, d), lambda i: (i, 0)),
        out_shape=jax.ShapeDtypeStruct((t, d), F32),
        compiler_params=_cparams(("parallel",)),
        name="final_norm",
    )(x2, g)


def _swap_halves(w):
    half = w.shape[-1] // 2
    return jnp.concatenate([w[..., half:], w[..., :half]], axis=-1)


def _head_block(nope, rope):
    pad = jnp.zeros(rope.shape[:-1] + (HEAD_PAD - MLA_NOPE - MLA_ROPE,), rope.dtype)
    blk = jnp.concatenate([nope, rope, pad], axis=-1)
    return blk.reshape(blk.shape[:-2] + (blk.shape[-2] * HEAD_PAD,))


def _mla_weights(w_in, w_uq, w_ukv):
    d = w_in.shape[0]
    r0 = MLA_Q_LORA + MLA_KV_LORA
    w_kr = w_in[:, r0:]
    z_nope = jnp.zeros((d, 1, MLA_NOPE), w_in.dtype)
    win = jnp.concatenate([w_in[:, :r0], _head_block(z_nope, w_kr[:, None, :]),
                           _head_block(z_nope, _swap_halves(w_kr)[:, None, :])], axis=1)
    uq = w_uq.reshape(MLA_Q_LORA, MLA_HEADS, MLA_NOPE + MLA_ROPE)
    uq_nope, uq_rope = uq[..., :MLA_NOPE], uq[..., MLA_NOPE:]
    wuq = jnp.concatenate([_head_block(uq_nope, uq_rope),
                           _head_block(jnp.zeros_like(uq_nope), _swap_halves(uq_rope))], axis=1)
    ukv = w_ukv.reshape(MLA_KV_LORA, MLA_HEADS, MLA_NOPE + MLA_V)
    uk, uv = ukv[..., :MLA_NOPE], ukv[..., MLA_NOPE:]
    wk = _head_block(uk, jnp.zeros(uk.shape[:-1] + (MLA_ROPE,), uk.dtype))
    wukv = jnp.concatenate([wk, uv.reshape(MLA_KV_LORA, MLA_HEADS * MLA_V)], axis=1)
    return win.astype(BF16), wuq.astype(BF16), wukv.astype(BF16)


def _rope_tables(positions):
    inv_freq = ROPE_THETA ** (-jnp.arange(0, MLA_ROPE, 2, dtype=F32) / MLA_ROPE)
    ang = positions.astype(F32).reshape(-1)[:, None] * inv_freq
    cos, sin = jnp.cos(ang), jnp.sin(ang)
    t = cos.shape[0]
    lead = jnp.zeros((t, ROPE_OFF), F32)
    tail = jnp.zeros((t, HEAD_PAD - ROPE_OFF - MLA_ROPE), F32)
    scale = (MLA_NOPE + MLA_ROPE) ** -0.5
    ck = jnp.concatenate([lead, cos, cos, tail], axis=1)
    sk = jnp.concatenate([lead, -sin, sin, tail], axis=1)
    cq = jnp.concatenate([lead + 1.0, cos, cos, tail], axis=1) * scale
    sq = sk * scale
    return cq, sq, ck, sk


def _pad_lanes(v, fill=0.0):
    return jnp.pad(v, [(0, 0)] * (v.ndim - 1) + [(0, LANES - v.shape[-1])], constant_values=fill)


def kernel(x, c, positions, ada_w, ada_b, norm_g, mla_w_in, mla_q_norm, mla_kv_norm, mla_w_uq, mla_w_ukv, mla_w_out, ssm_w_in, ssm_conv_w, ssm_conv_b, ssm_dt_bias, ssm_a_log, ssm_d, ssm_norm, ssm_w_out, ffn_w_gate, ffn_w_up, ffn_w_down, moe_w_router, moe_w_gate, moe_w_up, moe_w_down, final_norm):
    bsz, seq, d = x.shape
    depth = ada_w.shape[0]
    d_inner = ssm_norm.shape[1]
    cdim = ssm_conv_w.shape[2]
    t = bsz * seq
    x2 = x.reshape(t, d)

    c_pad = jnp.pad(c, ((0, SUBLANES - bsz), (0, 0)))
    mod = _adaln(c_pad, ada_w, ada_b)[:, :bsz].reshape(depth, bsz, 6, 1, d)
    cq, sq, ck, sk = _rope_tables(positions)
    ones_rw = jnp.ones((t, LANES), F32)

    for i in range(depth):
        j = i // 2
        sh1, sc1, g1, sh2, sc2, g2 = [mod[i, :, k] for k in range(6)]
        gain1 = norm_g[i, 0][None, :]
        gain2 = norm_g[i, 1][None, :]
        if i % 2 == 0:
            win, wuq, wukv = _mla_weights(mla_w_in[j], mla_w_uq[j], mla_w_ukv[j])
            q, k, v = _mla_in(x2, gain1, sh1, sc1, win, mla_q_norm[j][None, :], mla_kv_norm[j][None, :],
                              wuq, wukv, cq, sq, ck, sk, seq)
            o = _attention(q, k, v, bsz, seq)
            x2 = _proj_residual(o, mla_w_out[j].astype(BF16), x2, g1, seq)
            x2 = _ffn(x2, gain2, sh2, sc2, g2, ones_rw, ffn_w_gate[j][None].astype(BF16),
                      ffn_w_up[j][None].astype(BF16), ffn_w_down[j][None].astype(BF16), seq, False)
        else:
            w_in = ssm_w_in[j]
            wz = w_in[:, :d_inner].astype(BF16)
            wx = w_in[:, d_inner:d_inner + cdim].astype(BF16)
            wdt = _pad_lanes(w_in[:, d_inner + cdim:]).astype(BF16)
            z, xbc, dt = _ssm_in(x2, gain1, sh1, sc1, wz, wx, wdt, seq)
            yn = _ssd(z, xbc, dt, jnp.pad(ssm_conv_w[j], ((0, SUBLANES - SSM_CONV), (0, 0))),
                      ssm_conv_b[j][None, :], _pad_lanes(ssm_dt_bias[j][None, :]),
                      _pad_lanes(ssm_a_log[j][None, :]), jnp.repeat(ssm_d[j], SSM_HEADDIM)[None, :],
                      ssm_norm[j][None, :], bsz, seq)
            x2 = _proj_residual(yn, ssm_w_out[j].astype(BF16), x2, g1, seq)
            wr = _pad_lanes(moe_w_router[j])
            wr_hi = wr.astype(BF16)
            wr_lo = (wr - wr_hi.astype(F32)).astype(BF16)
            rw = _router(x2, gain2, sh2, sc2, wr_hi, wr_lo, seq)
            x2 = _ffn(x2, gain2, sh2, sc2, g2, rw, moe_w_gate[j].astype(BF16), moe_w_up[j].astype(BF16),
                      moe_w_down[j].astype(BF16), seq, True)
    return _final_norm(x2, final_norm[None, :], seq).reshape(bsz, seq, d)
```

```python
import functools

import jax
import jax.numpy as jnp
from jax import lax
from jax.experimental import pallas as pl
from jax.experimental.pallas import tpu as pltpu

F32 = jnp.float32
BF16 = jnp.bfloat16

EPS = 1e-6
CHUNK = 64
MLA_HEADS = 16
MLA_NOPE = 64
MLA_ROPE = 32
MLA_V = 64
MLA_Q_LORA = 384
MLA_KV_LORA = 256
ROPE_THETA = 10000.0
SSM_HEADDIM = 64
SSM_GROUPS = 4
SSM_STATE = 128
SSM_CONV = 4
N_EXPERTS = 8

LANES = 128
SUBLANES = 8
VMEM_LIMIT = 56 * 1024 * 1024

ROW_TILE = 512
FFN_ROW_TILE = 1024
FFN_COL_TILE = 256
ATTN_TILE = 512
ATTN_STRIP = 32
ATTN_HEADS_PER_STEP = 8
ATTN_UNROLL_HEADS = 4
SSD_BLOCK = 512
ADA_COL_TILE = 1536
DISPATCH_TILE = 256
MOE_ROW_TILE = 512
DMA_UNROLL = 8

HEAD_PAD = LANES
V_PAD = LANES
ROPE_OFF = MLA_NOPE
NEG_BIG = -1e30
LOG2E = 1.4426950408889634


def _cparams(sem):
    return pltpu.CompilerParams(dimension_semantics=sem, vmem_limit_bytes=VMEM_LIMIT)


def _rmsnorm_mod(x, g, sh, sc):
    y = x * lax.rsqrt(jnp.mean(x * x, axis=-1, keepdims=True) + EPS) * g
    return y * (1.0 + sc) + sh


def _sigmoid(v):
    return 1.0 / (1.0 + jnp.exp(-v))


def _split3(v):
    hi = v.astype(BF16)
    r1 = v - hi.astype(F32)
    mid = r1.astype(BF16)
    lo = (r1 - mid.astype(F32)).astype(BF16)
    return hi, mid, lo


def _dot(a, b):
    return jnp.dot(a, b, preferred_element_type=F32)


def _dot_nt(a, b):
    return lax.dot_general(a, b, (((1,), (1,)), ((), ())), preferred_element_type=F32)


def _adaln_kernel(c_ref, w_ref, b_ref, o_ref):
    c = c_ref[...]
    cond = (c * _sigmoid(c)).astype(BF16)
    o_ref[0] = _dot(cond, w_ref[0].astype(BF16)) + b_ref[0]


def _adaln(c_pad, ada_w, ada_b):
    depth, d, n = ada_w.shape
    bp = c_pad.shape[0]
    tn = min(ADA_COL_TILE, n)
    return pl.pallas_call(
        _adaln_kernel,
        grid=(depth, n // tn),
        in_specs=[
            pl.BlockSpec((bp, d), lambda l, j: (0, 0)),
            pl.BlockSpec((1, d, tn), lambda l, j: (l, 0, j)),
            pl.BlockSpec((1, 1, tn), lambda l, j: (l, 0, j)),
        ],
        out_specs=pl.BlockSpec((1, bp, tn), lambda l, j: (l, 0, j)),
        out_shape=jax.ShapeDtypeStruct((depth, bp, n), F32),
        compiler_params=_cparams(("parallel", "parallel")),
        name="adaln",
    )(c_pad, ada_w, ada_b.reshape(depth, 1, n))


def _mla_in_kernel(x_ref, g_ref, sh_ref, sc_ref, win_ref, qn_ref, kvn_ref, wuq_ref, wukv_ref,
                   cq_ref, sq_ref, ck_ref, sk_ref, q_out, k_out, v_out):
    h = _rmsnorm_mod(x_ref[...], g_ref[...], sh_ref[0], sc_ref[0]).astype(BF16)
    proj = _dot(h, win_ref[...])
    cq = proj[:, :MLA_Q_LORA]
    cq = (cq * lax.rsqrt(jnp.mean(cq * cq, axis=-1, keepdims=True) + EPS) * qn_ref[...]).astype(BF16)
    ckv = proj[:, MLA_Q_LORA:MLA_Q_LORA + MLA_KV_LORA]
    ckv = (ckv * lax.rsqrt(jnp.mean(ckv * ckv, axis=-1, keepdims=True) + EPS) * kvn_ref[...]).astype(BF16)
    r0 = MLA_Q_LORA + MLA_KV_LORA
    kr = proj[:, r0:r0 + HEAD_PAD] * ck_ref[...] + proj[:, r0 + HEAD_PAD:r0 + 2 * HEAD_PAD] * sk_ref[...]
    nq = MLA_HEADS * HEAD_PAD
    qa = _dot(cq, wuq_ref[:, :nq])
    qb = _dot(cq, wuq_ref[:, nq:])
    kv = _dot(ckv, wukv_ref[...])
    cqt = cq_ref[...]
    sqt = sq_ref[...]
    for hd in range(MLA_HEADS):
        sl = slice(hd * HEAD_PAD, (hd + 1) * HEAD_PAD)
        q_out[:, sl] = (qa[:, sl] * cqt + qb[:, sl] * sqt).astype(BF16)
        k_out[:, sl] = (kv[:, sl] + kr).astype(BF16)
    ones_half = jnp.where(lax.broadcasted_iota(jnp.int32, (1, V_PAD), 1) >= MLA_V, 1.0, 0.0)
    for hd in range(MLA_HEADS):
        sl = slice(hd * V_PAD, (hd + 1) * V_PAD)
        v_out[:, sl] = (kv[:, nq + hd * V_PAD:nq + (hd + 1) * V_PAD] + ones_half).astype(BF16)


def _mla_in(x2, g, sh, sc, win, qn, kvn, wuq, wukv, cq, sq, ck, sk, seq):
    t, d = x2.shape
    tm = min(ROW_TILE, seq)
    tpb = seq // tm
    nq = MLA_HEADS * HEAD_PAD
    nv = MLA_HEADS * V_PAD
    row = lambda i: (i, 0)
    fixed = lambda i: (0, 0)
    per_b = lambda i: (i // tpb, 0, 0)
    return pl.pallas_call(
        _mla_in_kernel,
        grid=(t // tm,),
        in_specs=[
            pl.BlockSpec((tm, d), row),
            pl.BlockSpec((1, d), fixed),
            pl.BlockSpec((1, 1, d), per_b),
            pl.BlockSpec((1, 1, d), per_b),
            pl.BlockSpec(win.shape, fixed),
            pl.BlockSpec(qn.shape, fixed),
            pl.BlockSpec(kvn.shape, fixed),
            pl.BlockSpec(wuq.shape, fixed),
            pl.BlockSpec(wukv.shape, fixed),
            pl.BlockSpec((tm, HEAD_PAD), row),
            pl.BlockSpec((tm, HEAD_PAD), row),
            pl.BlockSpec((tm, HEAD_PAD), row),
            pl.BlockSpec((tm, HEAD_PAD), row),
        ],
        out_specs=[
            pl.BlockSpec((tm, nq), row),
            pl.BlockSpec((tm, nq), row),
            pl.BlockSpec((tm, nv), row),
        ],
        out_shape=[
            jax.ShapeDtypeStruct((t, nq), BF16),
            jax.ShapeDtypeStruct((t, nq), BF16),
            jax.ShapeDtypeStruct((t, nv), BF16),
        ],
        compiler_params=_cparams(("parallel",)),
        name="mla_in",
    )(x2, g, sh, sc, win, qn, kvn, wuq, wukv, cq, sq, ck, sk)


def _attn_kernel(qi_tab, kj_tab, q_ref, k_ref, v_ref, o_ref, m_sc, acc_sc, s_sc, p_sc, a_sc):
    t = pl.program_id(2)
    qi = qi_tab[t]
    kj = kj_tab[t]
    tq = q_ref.shape[1]
    tk = k_ref.shape[1]
    nlt = tk // LANES
    nstrip = tq // ATTN_STRIP
    nun = s_sc.shape[0]
    ngroup = q_ref.shape[2] // (nun * HEAD_PAD)

    @pl.when(kj == 0)
    def _():
        m_sc[...] = jnp.full(m_sc.shape, NEG_BIG, F32)
        acc_sc[...] = jnp.zeros(acc_sc.shape, F32)

    def visible_tiles(i, diagonal):
        if not diagonal:
            return nlt, 0
        rc = (i * ATTN_STRIP) // CHUNK
        per_tile = LANES // CHUNK
        return (rc + 1) // per_tile, ((rc + 1) % per_tile) * CHUNK

    def qk(g, u):
        c0 = pl.multiple_of((g * nun + u) * HEAD_PAD, HEAD_PAD)
        s_sc[u] = _dot_nt(q_ref[0, :, pl.ds(c0, HEAD_PAD)], k_ref[0, :, pl.ds(c0, HEAD_PAD)])

    def softmax(g, u, diagonal):
        hd = g * nun + u
        lane = lax.broadcasted_iota(jnp.int32, (ATTN_STRIP, LANES), 1)
        for i in range(nstrip):
            r = slice(i * ATTN_STRIP, (i + 1) * ATTN_STRIP)
            full, part = visible_tiles(i, diagonal)
            mx = None
            for j in range(full + (1 if part else 0)):
                sj = s_sc[u, r, j * LANES:(j + 1) * LANES]
                if j == full:
                    sj = jnp.where(lane < part, sj, NEG_BIG)
                mx = sj if mx is None else jnp.maximum(mx, sj)
            m_prev = m_sc[hd, r, :]
            m_new = jnp.maximum(m_prev, jnp.max(mx, axis=-1, keepdims=True))
            a_sc[u, r, :] = jnp.exp2(m_prev - m_new)
            m_sc[hd, r, :] = m_new
        for i in range(nstrip):
            r = slice(i * ATTN_STRIP, (i + 1) * ATTN_STRIP)
            full, part = visible_tiles(i, diagonal)
            m_new = m_sc[hd, r, :]
            for j in range(nlt):
                c = slice(j * LANES, (j + 1) * LANES)
                if j < full or (j == full and part):
                    p = jnp.exp2(s_sc[u, r, c] - m_new)
                    if j == full:
                        p = jnp.where(lane < part, p, 0.0)
                    p_sc[u, r, c] = p.astype(BF16)
                else:
                    p_sc[u, r, c] = jnp.zeros((ATTN_STRIP, LANES), BF16)

    def pv(g, u):
        hd = g * nun + u
        c0 = pl.multiple_of(hd * V_PAD, V_PAD)
        acc_sc[hd] = acc_sc[hd] * a_sc[u] + _dot(p_sc[u], v_ref[0, :, pl.ds(c0, V_PAD)])

    def group_step(g, diagonal):
        for stage in range(nun + 2):
            if stage < nun:
                qk(g, stage)
            if 0 <= stage - 1 < nun:
                softmax(g, stage - 1, diagonal)
            if 0 <= stage - 2 < nun:
                pv(g, stage - 2)

    @pl.when(kj < qi)
    def _():
        def body(g, c):
            group_step(g, False)
            return c
        lax.fori_loop(0, ngroup, body, 0)

    @pl.when(kj == qi)
    def _():
        left = lax.broadcasted_iota(jnp.int32, (tq, LANES), 1) < MLA_V

        def body(g, c):
            group_step(g, True)
            for u in range(0, nun, 2):
                outs = []
                for hd in (g * nun + u, g * nun + u + 1):
                    acc = acc_sc[hd]
                    outs.append(acc / pltpu.roll(acc, MLA_V, axis=1))
                o0 = pl.multiple_of((g * nun + u) * MLA_V, 2 * MLA_V)
                o_ref[0, :, pl.ds(o0, 2 * MLA_V)] = jnp.where(
                    left, outs[0], pltpu.roll(outs[1], MLA_V, axis=1)).astype(o_ref.dtype)
            return c
        lax.fori_loop(0, ngroup, body, 0)


def _attention(q, k, v, bsz, seq):
    tile = min(ATTN_TILE, seq)
    hps = ATTN_HEADS_PER_STEP
    nun = ATTN_UNROLL_HEADS
    qi_list, kj_list = [], []
    for a in range(seq // tile):
        for b in range(a + 1):
            qi_list.append(a)
            kj_list.append(b)
    qi_tab = jnp.asarray(qi_list, jnp.int32)
    kj_tab = jnp.asarray(kj_list, jnp.int32)
    q3 = q.reshape(bsz, seq, MLA_HEADS * HEAD_PAD)
    k3 = k.reshape(bsz, seq, MLA_HEADS * HEAD_PAD)
    v3 = v.reshape(bsz, seq, MLA_HEADS * V_PAD)
    grid_spec = pltpu.PrefetchScalarGridSpec(
        num_scalar_prefetch=2,
        grid=(bsz, MLA_HEADS // hps, len(qi_list)),
        in_specs=[
            pl.BlockSpec((1, tile, hps * HEAD_PAD), lambda b, hg, t, qt, kt: (b, qt[t], hg)),
            pl.BlockSpec((1, tile, hps * HEAD_PAD), lambda b, hg, t, qt, kt: (b, kt[t], hg)),
            pl.BlockSpec((1, tile, hps * V_PAD), lambda b, hg, t, qt, kt: (b, kt[t], hg)),
        ],
        out_specs=pl.BlockSpec((1, tile, hps * MLA_V), lambda b, hg, t, qt, kt: (b, qt[t], hg)),
        scratch_shapes=[
            pltpu.VMEM((hps, tile, LANES), F32),
            pltpu.VMEM((hps, tile, V_PAD), F32),
            pltpu.VMEM((nun, tile, tile), F32),
            pltpu.VMEM((nun, tile, tile), BF16),
            pltpu.VMEM((nun, tile, LANES), F32),
        ],
    )
    o = pl.pallas_call(
        _attn_kernel,
        grid_spec=grid_spec,
        out_shape=jax.ShapeDtypeStruct((bsz, seq, MLA_HEADS * MLA_V), BF16),
        compiler_params=_cparams(("parallel", "parallel", "arbitrary")),
        name="mla_attention",
    )(qi_tab, kj_tab, q3, k3, v3)
    return o.reshape(bsz * seq, MLA_HEADS * MLA_V)


def _proj_res_kernel(a_ref, w_ref, x_ref, gate_ref, o_ref):
    o_ref[...] = x_ref[...] + gate_ref[0] * _dot(a_ref[...], w_ref[...])


def _proj_residual(a, w, x2, gate, seq):
    t, d = x2.shape
    kdim = a.shape[1]
    tm = min(ROW_TILE, seq)
    tpb = seq // tm
    return pl.pallas_call(
        _proj_res_kernel,
        grid=(t // tm,),
        in_specs=[
            pl.BlockSpec((tm, kdim), lambda i: (i, 0)),
            pl.BlockSpec((kdim, d), lambda i: (0, 0)),
            pl.BlockSpec((tm, d), lambda i: (i, 0)),
            pl.BlockSpec((1, 1, d), lambda i: (i // tpb, 0, 0)),
        ],
        out_specs=pl.BlockSpec((tm, d), lambda i: (i, 0)),
        out_shape=jax.ShapeDtypeStruct((t, d), F32),
        compiler_params=_cparams(("parallel",)),
        name="proj_residual",
    )(a, w, x2, gate)


def _ffn_kernel(x_ref, g_ref, sh_ref, sc_ref, gate_ref, wg_ref, wu_ref, wd_ref, o_ref, h_sc, acc_sc):
    f = pl.program_id(1)

    @pl.when(f == 0)
    def _():
        h_sc[...] = _rmsnorm_mod(x_ref[...], g_ref[...], sh_ref[0], sc_ref[0]).astype(BF16)
        acc_sc[...] = jnp.zeros(acc_sc.shape, F32)

    h = h_sc[...]
    gt = _dot(h, wg_ref[...])
    up = _dot(h, wu_ref[...])
    acc_sc[...] += _dot((gt * _sigmoid(gt) * up).astype(BF16), wd_ref[...])

    @pl.when(f == pl.num_programs(1) - 1)
    def _():
        o_ref[...] = x_ref[...] + gate_ref[0] * acc_sc[...]


def _ffn(x2, g, sh, sc, gate, wg, wu, wd, seq):
    t, d = x2.shape
    dff = wg.shape[1]
    tm = min(FFN_ROW_TILE, seq)
    tf = min(FFN_COL_TILE, dff)
    tpb = seq // tm
    per_b = lambda i, f: (i // tpb, 0, 0)
    return pl.pallas_call(
        _ffn_kernel,
        grid=(t // tm, dff // tf),
        in_specs=[
            pl.BlockSpec((tm, d), lambda i, f: (i, 0)),
            pl.BlockSpec((1, d), lambda i, f: (0, 0)),
            pl.BlockSpec((1, 1, d), per_b),
            pl.BlockSpec((1, 1, d), per_b),
            pl.BlockSpec((1, 1, d), per_b),
            pl.BlockSpec((d, tf), lambda i, f: (0, f)),
            pl.BlockSpec((d, tf), lambda i, f: (0, f)),
            pl.BlockSpec((tf, d), lambda i, f: (f, 0)),
        ],
        out_specs=pl.BlockSpec((tm, d), lambda i, f: (i, 0)),
        out_shape=jax.ShapeDtypeStruct((t, d), F32),
        scratch_shapes=[pltpu.VMEM((tm, d), BF16), pltpu.VMEM((tm, d), F32)],
        compiler_params=_cparams(("parallel", "arbitrary")),
        name="swiglu",
    )(x2, g, sh, sc, gate, wg, wu, wd)


INFO_E1, INFO_E2, INFO_W1, INFO_W2, INFO_RANK1, INFO_RANK2 = range(6)


def _router_kernel(x_ref, g_ref, sh_ref, sc_ref, whi_ref, wlo_ref, tri_ref, info_ref, cnt_ref, carry_sc):
    @pl.when(pl.program_id(0) == 0)
    def _():
        carry_sc[...] = jnp.zeros(carry_sc.shape, F32)

    h = _rmsnorm_mod(x_ref[...], g_ref[...], sh_ref[0], sc_ref[0])
    h_hi = h.astype(BF16)
    h_lo = (h - h_hi.astype(F32)).astype(BF16)
    whi = whi_ref[...]
    logits = _dot(h_hi, whi) + _dot(h_lo, whi) + _dot(h_hi, wlo_ref[...])
    lane = lax.broadcasted_iota(jnp.int32, logits.shape, 1)
    neg = jnp.float32(-jnp.inf)
    lg = jnp.where(lane < N_EXPERTS, logits, neg)
    m1 = jnp.max(lg, axis=-1, keepdims=True)
    i1 = jnp.min(jnp.where(lg == m1, lane, LANES), axis=-1, keepdims=True)
    lg2 = jnp.where(lane == i1, neg, lg)
    m2 = jnp.max(lg2, axis=-1, keepdims=True)
    i2 = jnp.min(jnp.where(lg2 == m2, lane, LANES), axis=-1, keepdims=True)
    e2 = jnp.exp(m2 - m1)
    w1 = 1.0 / (1.0 + e2)
    w2 = e2 / (1.0 + e2)
    sel1 = lane == i1
    sel2 = lane == i2
    onehot = jnp.where(sel1 | sel2, 1.0, 0.0)
    before = carry_sc[0:1, :] + _dot(tri_ref[...], onehot.astype(BF16))
    rank1 = jnp.sum(jnp.where(sel1, before, 0.0), axis=-1, keepdims=True)
    rank2 = jnp.sum(jnp.where(sel2, before, 0.0), axis=-1, keepdims=True)
    carry_sc[0:1, :] = carry_sc[0:1, :] + jnp.sum(onehot, axis=0, keepdims=True)
    info = jnp.zeros(logits.shape, F32)
    for k, val in ((INFO_E1, i1.astype(F32)), (INFO_E2, i2.astype(F32)), (INFO_W1, w1), (INFO_W2, w2),
                   (INFO_RANK1, rank1), (INFO_RANK2, rank2)):
        info = jnp.where(lane == k, val, info)
    info_ref[...] = info
    cnt_ref[...] = carry_sc[...]


def _router(x2, g, sh, sc, whi, wlo, seq):
    t, d = x2.shape
    tm = min(ROW_TILE, seq)
    tpb = seq // tm
    per_b = lambda i: (i // tpb, 0, 0)
    r = jnp.arange(tm)
    tri = (r[None, :] < r[:, None]).astype(BF16)
    return pl.pallas_call(
        _router_kernel,
        grid=(t // tm,),
        in_specs=[
            pl.BlockSpec((tm, d), lambda i: (i, 0)),
            pl.BlockSpec((1, d), lambda i: (0, 0)),
            pl.BlockSpec((1, 1, d), per_b),
            pl.BlockSpec((1, 1, d), per_b),
            pl.BlockSpec((d, LANES), lambda i: (0, 0)),
            pl.BlockSpec((d, LANES), lambda i: (0, 0)),
            pl.BlockSpec((tm, tm), lambda i: (0, 0)),
        ],
        out_specs=[
            pl.BlockSpec((tm, LANES), lambda i: (i, 0)),
            pl.BlockSpec((SUBLANES, LANES), lambda i: (0, 0)),
        ],
        out_shape=[
            jax.ShapeDtypeStruct((t, LANES), F32),
            jax.ShapeDtypeStruct((SUBLANES, LANES), F32),
        ],
        scratch_shapes=[pltpu.VMEM((SUBLANES, LANES), F32)],
        compiler_params=_cparams(("arbitrary",)),
        name="moe_router",
    )(x2, g, sh, sc, whi, wlo, tri)


def _row_copy(src_ref, src_row, dst_ref, dst_row, sem):
    return pltpu.make_async_copy(src_ref.at[pl.ds(src_row, 1), :], dst_ref.at[pl.ds(dst_row, 1), :], sem)


def _load_row_indices(r1_ref, r2_ref, idx_sm, isem):
    c1 = pltpu.make_async_copy(r1_ref.at[0], idx_sm.at[0], isem.at[0])
    c2 = pltpu.make_async_copy(r2_ref.at[0], idx_sm.at[1], isem.at[1])
    c1.start()
    c2.start()
    return c1, c2


def _dispatch_kernel(x_ref, g_ref, sh_ref, sc_ref, r1_ref, r2_ref, xs_in_ref, xs_ref, h_sc, idx_sm, sem, isem):
    del xs_in_ref
    tm = x_ref.shape[0]
    c1, c2 = _load_row_indices(r1_ref, r2_ref, idx_sm, isem)
    h_sc[...] = _rmsnorm_mod(x_ref[...], g_ref[...], sh_ref[0], sc_ref[0])
    c1.wait()
    c2.wait()

    def copies(m):
        return (_row_copy(h_sc, m, xs_ref, idx_sm[0, 0, m], sem), _row_copy(h_sc, m, xs_ref, idx_sm[1, 0, m], sem))

    def issue(m, c):
        for cp in copies(m):
            cp.start()
        return c

    def drain(m, c):
        for cp in copies(m):
            cp.wait()
        return c

    lax.fori_loop(0, tm, issue, 0, unroll=DMA_UNROLL)
    lax.fori_loop(0, tm, drain, 0, unroll=DMA_UNROLL)


def _dispatch(x2, g, sh, sc, r1, r2, rows_pad, seq):
    t, d = x2.shape
    tm = min(DISPATCH_TILE, seq)
    tpb = seq // tm
    per_b = lambda i: (i // tpb, 0, 0)
    idx_spec = pl.BlockSpec((1, 1, tm), lambda i: (i, 0, 0))
    return pl.pallas_call(
        _dispatch_kernel,
        grid=(t // tm,),
        in_specs=[
            pl.BlockSpec((tm, d), lambda i: (i, 0)),
            pl.BlockSpec((1, d), lambda i: (0, 0)),
            pl.BlockSpec((1, 1, d), per_b),
            pl.BlockSpec((1, 1, d), per_b),
            idx_spec,
            idx_spec,
            pl.BlockSpec(memory_space=pl.ANY),
        ],
        out_specs=pl.BlockSpec(memory_space=pl.ANY),
        out_shape=jax.ShapeDtypeStruct((rows_pad, d), F32),
        scratch_shapes=[
            pltpu.VMEM((tm, d), F32),
            pltpu.SMEM((2, 1, tm), jnp.int32),
            pltpu.SemaphoreType.DMA,
            pltpu.SemaphoreType.DMA((2,)),
        ],
        input_output_aliases={6: 0},
        compiler_params=_cparams(("arbitrary",)),
        name="moe_dispatch",
    )(x2, g, sh, sc, r1.reshape(t // tm, 1, tm), r2.reshape(t // tm, 1, tm), jnp.zeros((rows_pad, d), F32))


def _expert_ffn_kernel(te_ref, nu_ref, x_ref, wg_ref, wu_ref, wd_ref, o_ref, h_sc, acc_sc):
    del te_ref
    f = pl.program_id(1)

    @pl.when(pl.program_id(0) < nu_ref[0])
    def _():
        @pl.when(f == 0)
        def _():
            h_sc[...] = x_ref[...].astype(BF16)
            acc_sc[...] = jnp.zeros(acc_sc.shape, F32)

        h = h_sc[...]
        gt = _dot(h, wg_ref[0])
        up = _dot(h, wu_ref[0])
        acc_sc[...] += _dot((gt * _sigmoid(gt) * up).astype(BF16), wd_ref[0])

        @pl.when(f == pl.num_programs(1) - 1)
        def _():
            o_ref[...] = acc_sc[...]

    @pl.when((pl.program_id(0) >= nu_ref[0]) & (f == pl.num_programs(1) - 1))
    def _():
        o_ref[...] = jnp.zeros(o_ref.shape, F32)


def _expert_ffn(xs, tile_expert, n_used, wg, wu, wd):
    rows_pad, d = xs.shape
    dff = wg.shape[2]
    tm = MOE_ROW_TILE
    tf = min(FFN_COL_TILE, dff)
    row = lambda i, f, te, nu: (jnp.minimum(i, nu[0] - 1), 0)
    grid_spec = pltpu.PrefetchScalarGridSpec(
        num_scalar_prefetch=2,
        grid=(rows_pad // tm, dff // tf),
        in_specs=[
            pl.BlockSpec((tm, d), row),
            pl.BlockSpec((1, d, tf), lambda i, f, te, nu: (te[i], 0, f)),
            pl.BlockSpec((1, d, tf), lambda i, f, te, nu: (te[i], 0, f)),
            pl.BlockSpec((1, tf, d), lambda i, f, te, nu: (te[i], f, 0)),
        ],
        out_specs=pl.BlockSpec((tm, d), lambda i, f, te, nu: (i, 0)),
        scratch_shapes=[pltpu.VMEM((tm, d), BF16), pltpu.VMEM((tm, d), F32)],
    )
    return pl.pallas_call(
        _expert_ffn_kernel,
        grid_spec=grid_spec,
        out_shape=jax.ShapeDtypeStruct((rows_pad, d), F32),
        compiler_params=_cparams(("arbitrary", "arbitrary")),
        name="moe_expert_ffn",
    )(tile_expert, n_used, xs, wg, wu, wd)


def _combine_kernel(x_ref, gate_ref, info_ref, r1_ref, r2_ref, fg_ref, ys_ref, o_ref, y1_sc, y2_sc, idx_sm, sem,
                    isem, *, final_norm):
    tm = x_ref.shape[0]
    c1, c2 = _load_row_indices(r1_ref, r2_ref, idx_sm, isem)
    c1.wait()
    c2.wait()

    def copies(m):
        return (_row_copy(ys_ref, idx_sm[0, 0, m], y1_sc, m, sem), _row_copy(ys_ref, idx_sm[1, 0, m], y2_sc, m, sem))

    def issue(m, c):
        for cp in copies(m):
            cp.start()
        return c

    def drain(m, c):
        for cp in copies(m):
            cp.wait()
        return c

    lax.fori_loop(0, tm, issue, 0, unroll=DMA_UNROLL)
    lax.fori_loop(0, tm, drain, 0, unroll=DMA_UNROLL)
    info = info_ref[...]
    w1 = info[:, INFO_W1:INFO_W1 + 1]
    w2 = info[:, INFO_W2:INFO_W2 + 1]
    out = x_ref[...] + gate_ref[0] * (w1 * y1_sc[...] + w2 * y2_sc[...])
    if final_norm:
        out = out * lax.rsqrt(jnp.mean(out * out, axis=-1, keepdims=True) + EPS) * fg_ref[...]
    o_ref[...] = out


def _combine(x2, gate, info, r1, r2, ys, final_gain, seq, final_norm):
    t, d = x2.shape
    tm = min(DISPATCH_TILE, seq)
    tpb = seq // tm
    idx_spec = pl.BlockSpec((1, 1, tm), lambda i: (i, 0, 0))
    return pl.pallas_call(
        functools.partial(_combine_kernel, final_norm=final_norm),
        grid=(t // tm,),
        in_specs=[
            pl.BlockSpec((tm, d), lambda i: (i, 0)),
            pl.BlockSpec((1, 1, d), lambda i: (i // tpb, 0, 0)),
            pl.BlockSpec((tm, LANES), lambda i: (i, 0)),
            idx_spec,
            idx_spec,
            pl.BlockSpec((1, d), lambda i: (0, 0)),
            pl.BlockSpec(memory_space=pl.ANY),
        ],
        out_specs=pl.BlockSpec((tm, d), lambda i: (i, 0)),
        out_shape=jax.ShapeDtypeStruct((t, d), F32),
        scratch_shapes=[
            pltpu.VMEM((tm, d), F32),
            pltpu.VMEM((tm, d), F32),
            pltpu.SMEM((2, 1, tm), jnp.int32),
            pltpu.SemaphoreType.DMA,
            pltpu.SemaphoreType.DMA((2,)),
        ],
        compiler_params=_cparams(("arbitrary",)),
        name="moe_combine",
    )(x2, gate, info, r1.reshape(t // tm, 1, tm), r2.reshape(t // tm, 1, tm), final_gain, ys)


def _moe_plan(info, counts, n_tokens):
    tm = MOE_ROW_TILE
    cnt = counts[0, :N_EXPERTS].astype(jnp.int32)
    padded = ((cnt + tm - 1) // tm) * tm
    ends = jnp.cumsum(padded)
    starts = ends - padded
    e1 = info[:, INFO_E1].astype(jnp.int32)
    e2 = info[:, INFO_E2].astype(jnp.int32)
    r1 = starts[e1] + info[:, INFO_RANK1].astype(jnp.int32)
    r2 = starts[e2] + info[:, INFO_RANK2].astype(jnp.int32)
    n_tiles = (2 * n_tokens) // tm + N_EXPERTS
    n_used = (ends[-1] // tm).reshape(1)
    tile_start = jnp.minimum(jnp.arange(n_tiles, dtype=jnp.int32), n_used - 1) * tm
    tile_expert = jnp.sum(tile_start[:, None] >= ends[None, :], axis=1).astype(jnp.int32)
    return r1, r2, tile_expert, n_used.astype(jnp.int32), n_tiles * tm


def _ssm_in_kernel(x_ref, g_ref, sh_ref, sc_ref, wz_ref, wx_ref, wdt_ref, z_out, xbc_out, dt_out):
    h = _rmsnorm_mod(x_ref[...], g_ref[...], sh_ref[0], sc_ref[0]).astype(BF16)
    z_out[...] = _dot(h, wz_ref[...]).astype(BF16)
    xbc_out[...] = _dot(h, wx_ref[...]).astype(BF16)
    dt_out[...] = _dot(h, wdt_ref[...])


def _ssm_in(x2, g, sh, sc, wz, wx, wdt, seq):
    t, d = x2.shape
    tm = min(ROW_TILE, seq)
    tpb = seq // tm
    per_b = lambda i: (i // tpb, 0, 0)
    fixed = lambda i: (0, 0)
    row = lambda i: (i, 0)
    return pl.pallas_call(
        _ssm_in_kernel,
        grid=(t // tm,),
        in_specs=[
            pl.BlockSpec((tm, d), row),
            pl.BlockSpec((1, d), fixed),
            pl.BlockSpec((1, 1, d), per_b),
            pl.BlockSpec((1, 1, d), per_b),
            pl.BlockSpec(wz.shape, fixed),
            pl.BlockSpec(wx.shape, fixed),
            pl.BlockSpec(wdt.shape, fixed),
        ],
        out_specs=[
            pl.BlockSpec((tm, wz.shape[1]), row),
            pl.BlockSpec((tm, wx.shape[1]), row),
            pl.BlockSpec((tm, LANES), row),
        ],
        out_shape=[
            jax.ShapeDtypeStruct((t, wz.shape[1]), BF16),
            jax.ShapeDtypeStruct((t, wx.shape[1]), BF16),
            jax.ShapeDtypeStruct((t, LANES), F32),
        ],
        compiler_params=_cparams(("parallel",)),
        name="ssm_in",
    )(x2, g, sh, sc, wz, wx, wdt)


def _ssd_kernel(z_ref, xbc_ref, dt_ref, cw_ref, cb_ref, dtb_ref, alog_ref, dskip_ref, ng_ref,
                tri_ref, exp_ref, eye_ref, o_ref, ubuf, acs_sc, dtx_sc, state_sc, *, d_inner):
    blk = z_ref.shape[0]
    gw = d_inner // SSM_GROUPS
    gn = SSM_GROUPS * SSM_STATE
    pair = 2 * SSM_HEADDIM

    @pl.when(pl.program_id(1) == 0)
    def _():
        ubuf[0:SUBLANES, :] = jnp.zeros((SUBLANES, ubuf.shape[1]), F32)
        state_sc[...] = jnp.zeros(state_sc.shape, F32)

    ubuf[SUBLANES:SUBLANES + blk, :] = xbc_ref[...].astype(F32)

    lane = lax.broadcasted_iota(jnp.int32, (1, LANES), 1)
    a = jnp.where(lane < d_inner // SSM_HEADDIM, -jnp.exp(alog_ref[...]), 0.0)
    v = dt_ref[...] + dtb_ref[...]
    dt = jnp.maximum(v, 0.0) + jnp.log(1.0 + jnp.exp(-jnp.abs(v)))
    tri = tri_ref[...]
    acs = sum(_dot(tri, part) for part in _split3(dt * a))
    expand = exp_ref[...]
    acs_sc[...] = sum(_dot(part, expand) for part in _split3(acs))
    dtx_sc[...] = sum(_dot(part, expand) for part in _split3(dt))

    cw = cw_ref[...]
    cbias = cb_ref[...]
    dskip = dskip_ref[...]
    ngain = ng_ref[...]
    eye = eye_ref[...]
    row_i = lax.broadcasted_iota(jnp.int32, (CHUNK, pair), 0)
    col_i = lax.broadcasted_iota(jnp.int32, (CHUNK, pair), 1)
    col_s = jnp.where(col_i >= SSM_HEADDIM, col_i - SSM_HEADDIM, col_i)
    diag = row_i == col_s
    causal = col_s <= row_i
    first_head = col_i < SSM_HEADDIM

    def chunk(c, carry):
        r0 = pl.multiple_of(c * CHUNK, CHUNK)
        win = ubuf[pl.ds(r0, CHUNK + SUBLANES), :]
        u = cbias
        for j in range(SSM_CONV):
            lo = SUBLANES - (SSM_CONV - 1) + j
            u = u + cw[j:j + 1, :] * win[lo:lo + CHUNK, :]
        xbc = u * _sigmoid(u)
        xs = xbc[:, :d_inner]
        bm = xbc[:, d_inner:d_inner + gn].astype(BF16)
        cm = xbc[:, d_inner + gn:].astype(BF16)
        ae = acs_sc[pl.ds(r0, CHUNK), :]
        xdt = xs * dtx_sc[pl.ds(r0, CHUNK), :]
        last = ae[CHUNK - 1:CHUNK, :]
        from_start = jnp.exp(ae)
        xdte = (xdt * jnp.exp(last - ae)).astype(BF16)
        chunk_decay = jnp.exp(last)
        zc = z_ref[pl.ds(r0, CHUNK), :].astype(F32)
        zgate = zc * _sigmoid(zc)
        for g in range(SSM_GROUPS):
            bg = bm[:, g * SSM_STATE:(g + 1) * SSM_STATE]
            cg = cm[:, g * SSM_STATE:(g + 1) * SSM_STATE]
            gs = slice(g * gw, (g + 1) * gw)
            cbcb = _dot_nt(cg, jnp.concatenate([bg, bg], axis=0))
            s_prev = state_sc[g]
            y_off = _dot(cg, s_prev.astype(BF16)) * from_start[:, gs]
            bg_t = _dot_nt(eye, bg).astype(BF16)
            state_sc[g] = s_prev * chunk_decay[:, gs] + _dot(bg_t, xdte[:, gs])
            ys = []
            for pr in range(gw // pair):
                ps = slice(g * gw + pr * pair, g * gw + (pr + 1) * pair)
                dcol = ae[:, ps]
                drow = jnp.sum(jnp.where(diag, dcol, 0.0), axis=0, keepdims=True)
                decay = jnp.where(causal, jnp.exp(jnp.where(causal, dcol - drow, 0.0)), 0.0)
                m = (cbcb * decay).astype(BF16)
                xp = xdt[:, ps]
                bd = jnp.concatenate([jnp.where(first_head, xp, 0.0), jnp.where(first_head, 0.0, xp)],
                                     axis=0).astype(BF16)
                ys.append(_dot(m, bd) + y_off[:, pr * pair:(pr + 1) * pair])
            yg = (jnp.concatenate(ys, axis=1) + dskip[:, gs] * xs[:, gs]) * zgate[:, gs]
            yn = yg * lax.rsqrt(jnp.mean(yg * yg, axis=-1, keepdims=True) + EPS) * ngain[:, gs]
            o_ref[pl.ds(r0, CHUNK), gs] = yn.astype(o_ref.dtype)
        return carry

    lax.fori_loop(0, blk // CHUNK, chunk, 0)
    ubuf[0:SUBLANES, :] = ubuf[blk:blk + SUBLANES, :]


def _ssd(z, xbc, dt, cw, cb, dtb, alog, dskip, ng, bsz, seq):
    d_inner = z.shape[1]
    cdim = xbc.shape[1]
    blk = min(SSD_BLOCK, seq)
    nblk = seq // blk
    r = jnp.arange(blk)
    tri = (((r[:, None] // CHUNK) == (r[None, :] // CHUNK)) & (r[None, :] <= r[:, None])).astype(BF16)
    heads = d_inner // SSM_HEADDIM
    expand = (jnp.arange(LANES)[:, None] == (jnp.arange(d_inner)[None, :] // SSM_HEADDIM)).astype(BF16)
    eye = jnp.eye(SSM_STATE, dtype=BF16)
    del heads
    row = lambda b, i: (b * nblk + i, 0)
    fixed = lambda b, i: (0, 0)
    return pl.pallas_call(
        functools.partial(_ssd_kernel, d_inner=d_inner),
        grid=(bsz, nblk),
        in_specs=[
            pl.BlockSpec((blk, d_inner), row),
            pl.BlockSpec((blk, cdim), row),
            pl.BlockSpec((blk, LANES), row),
            pl.BlockSpec(cw.shape, fixed),
            pl.BlockSpec(cb.shape, fixed),
            pl.BlockSpec(dtb.shape, fixed),
            pl.BlockSpec(alog.shape, fixed),
            pl.BlockSpec(dskip.shape, fixed),
            pl.BlockSpec(ng.shape, fixed),
            pl.BlockSpec(tri.shape, fixed),
            pl.BlockSpec(expand.shape, fixed),
            pl.BlockSpec(eye.shape, fixed),
        ],
        out_specs=pl.BlockSpec((blk, d_inner), row),
        out_shape=jax.ShapeDtypeStruct((bsz * seq, d_inner), BF16),
        scratch_shapes=[
            pltpu.VMEM((blk + SUBLANES, cdim), F32),
            pltpu.VMEM((blk, d_inner), F32),
            pltpu.VMEM((blk, d_inner), F32),
            pltpu.VMEM((SSM_GROUPS, SSM_STATE, d_inner // SSM_GROUPS), F32),
        ],
        compiler_params=_cparams(("parallel", "arbitrary")),
        name="ssd",
    )(z, xbc, dt, cw, cb, dtb, alog, dskip, ng, tri, expand, eye)


def _final_norm_kernel(x_ref, g_ref, o_ref):
    x = x_ref[...]
    o_ref[...] = x * lax.rsqrt(jnp.mean(x * x, axis=-1, keepdims=True) + EPS) * g_ref[...]


def _final_norm(x2, g, seq):
    t, d = x2.shape
    tm = min(FFN_ROW_TILE, seq)
    return pl.pallas_call(
        _final_norm_kernel,
        grid=(t // tm,),
        in_specs=[pl.BlockSpec((tm, d), lambda i: (i, 0)), pl.BlockSpec((1, d), lambda i: (0, 0))],
        out_specs=pl.BlockSpec((tm, d), lambda i: (i, 0)),
        out_shape=jax.ShapeDtypeStruct((t, d), F32),
        compiler_params=_cparams(("parallel",)),
        name="final_norm",
    )(x2, g)


def _swap_halves(w):
    half = w.shape[-1] // 2
    return jnp.concatenate([w[..., half:], w[..., :half]], axis=-1)


def _head_block(nope, rope):
    pad = jnp.zeros(rope.shape[:-1] + (HEAD_PAD - MLA_NOPE - MLA_ROPE,), rope.dtype)
    blk = jnp.concatenate([nope, rope, pad], axis=-1)
    return blk.reshape(blk.shape[:-2] + (blk.shape[-2] * HEAD_PAD,))


def _mla_weights(w_in, w_uq, w_ukv):
    d = w_in.shape[0]
    r0 = MLA_Q_LORA + MLA_KV_LORA
    w_kr = w_in[:, r0:]
    z_nope = jnp.zeros((d, 1, MLA_NOPE), w_in.dtype)
    win = jnp.concatenate([w_in[:, :r0], _head_block(z_nope, w_kr[:, None, :]),
                           _head_block(z_nope, _swap_halves(w_kr)[:, None, :])], axis=1)
    uq = w_uq.reshape(MLA_Q_LORA, MLA_HEADS, MLA_NOPE + MLA_ROPE)
    uq_nope, uq_rope = uq[..., :MLA_NOPE], uq[..., MLA_NOPE:]
    wuq = jnp.concatenate([_head_block(uq_nope, uq_rope),
                           _head_block(jnp.zeros_like(uq_nope), _swap_halves(uq_rope))], axis=1)
    ukv = w_ukv.reshape(MLA_KV_LORA, MLA_HEADS, MLA_NOPE + MLA_V)
    uk, uv = ukv[..., :MLA_NOPE], ukv[..., MLA_NOPE:]
    wk = _head_block(uk, jnp.zeros(uk.shape[:-1] + (MLA_ROPE,), uk.dtype))
    wv = jnp.concatenate([uv, jnp.zeros(uv.shape[:-1] + (V_PAD - MLA_V,), uv.dtype)], axis=-1)
    wukv = jnp.concatenate([wk, wv.reshape(MLA_KV_LORA, MLA_HEADS * V_PAD)], axis=1)
    return win.astype(BF16), wuq.astype(BF16), wukv.astype(BF16)


def _rope_tables(positions):
    inv_freq = ROPE_THETA ** (-jnp.arange(0, MLA_ROPE, 2, dtype=F32) / MLA_ROPE)
    ang = positions.astype(F32).reshape(-1)[:, None] * inv_freq
    cos, sin = jnp.cos(ang), jnp.sin(ang)
    t = cos.shape[0]
    lead = jnp.zeros((t, ROPE_OFF), F32)
    tail = jnp.zeros((t, HEAD_PAD - ROPE_OFF - MLA_ROPE), F32)
    scale = (MLA_NOPE + MLA_ROPE) ** -0.5 * LOG2E
    ck =jnp.concatenate([lead, cos, cos, tail], axis=1)
    sk = jnp.concatenate([lead, -sin, sin, tail], axis=1)
    cq = jnp.concatenate([lead + 1.0, cos, cos, tail], axis=1) * scale
    sq = sk * scale
    return cq, sq, ck, sk


def _pad_lanes(v, fill=0.0):
    return jnp.pad(v, [(0, 0)] * (v.ndim - 1) + [(0, LANES - v.shape[-1])], constant_values=fill)


def kernel(x, c, positions, ada_w, ada_b, norm_g, mla_w_in, mla_q_norm, mla_kv_norm, mla_w_uq, mla_w_ukv, mla_w_out, ssm_w_in, ssm_conv_w, ssm_conv_b, ssm_dt_bias, ssm_a_log, ssm_d, ssm_norm, ssm_w_out, ffn_w_gate, ffn_w_up, ffn_w_down, moe_w_router, moe_w_gate, moe_w_up, moe_w_down, final_norm):
    bsz, seq, d = x.shape
    depth = ada_w.shape[0]
    d_inner = ssm_norm.shape[1]
    cdim = ssm_conv_w.shape[2]
    t = bsz * seq
    x2 = x.reshape(t, d)

    c_pad = jnp.pad(c, ((0, SUBLANES - bsz), (0, 0)))
    mod = _adaln(c_pad, ada_w, ada_b)[:, :bsz].reshape(depth, bsz, 6, 1, d)
    cq, sq, ck, sk = _rope_tables(positions)

    for i in range(depth):
        j = i // 2
        sh1, sc1, g1, sh2, sc2, g2 = [mod[i, :, k] for k in range(6)]
        gain1 = norm_g[i, 0][None, :]
        gain2 = norm_g[i, 1][None, :]
        if i % 2 == 0:
            win, wuq, wukv = _mla_weights(mla_w_in[j], mla_w_uq[j], mla_w_ukv[j])
            q, k, v = _mla_in(x2, gain1, sh1, sc1, win, mla_q_norm[j][None, :], mla_kv_norm[j][None, :],
                              wuq, wukv, cq, sq, ck, sk, seq)
            o = _attention(q, k, v, bsz, seq)
            x2 = _proj_residual(o, mla_w_out[j].astype(BF16), x2, g1, seq)
            x2 = _ffn(x2, gain2, sh2, sc2, g2, ffn_w_gate[j].astype(BF16), ffn_w_up[j].astype(BF16),
                      ffn_w_down[j].astype(BF16), seq)
        else:
            w_in = ssm_w_in[j]
            wz = w_in[:, :d_inner].astype(BF16)
            wx = w_in[:, d_inner:d_inner + cdim].astype(BF16)
            wdt = _pad_lanes(w_in[:, d_inner + cdim:]).astype(BF16)
            z, xbc, dt = _ssm_in(x2, gain1, sh1, sc1, wz, wx, wdt, seq)
            yn = _ssd(z, xbc, dt, jnp.pad(ssm_conv_w[j], ((0, SUBLANES - SSM_CONV), (0, 0))),
                      ssm_conv_b[j][None, :], _pad_lanes(ssm_dt_bias[j][None, :]),
                      _pad_lanes(ssm_a_log[j][None, :]), jnp.repeat(ssm_d[j], SSM_HEADDIM)[None, :],
                      ssm_norm[j][None, :], bsz, seq)
            x2 = _proj_residual(yn, ssm_w_out[j].astype(BF16), x2, g1, seq)
            wr = _pad_lanes(moe_w_router[j])
            wr_hi = wr.astype(BF16)
            wr_lo = (wr - wr_hi.astype(F32)).astype(BF16)
            info, counts = _router(x2, gain2, sh2, sc2, wr_hi, wr_lo, seq)
            r1, r2, tile_expert, n_used, rows_pad = _moe_plan(info, counts, t)
            xs = _dispatch(x2, gain2, sh2, sc2, r1, r2, rows_pad, seq)
            ys = _expert_ffn(xs, tile_expert, n_used, moe_w_gate[j].astype(BF16), moe_w_up[j].astype(BF16),
                             moe_w_down[j].astype(BF16))
            x2 = _combine(x2, g2, info, r1, r2, ys, final_norm[None, :], seq, i == depth - 1)
    if depth % 2 == 1:
        x2 = _final_norm(x2, final_norm[None, :], seq)
    return x2.reshape(bsz, seq, d)
```

```python
import functools

import jax
import jax.numpy as jnp
from jax import lax
from jax.experimental import pallas as pl
from jax.experimental.pallas import tpu as pltpu

F32 = jnp.float32
BF16 = jnp.bfloat16

EPS = 1e-6
CHUNK = 64
MLA_HEADS = 16
MLA_NOPE = 64
MLA_ROPE = 32
MLA_V = 64
MLA_Q_LORA = 384
MLA_KV_LORA = 256
ROPE_THETA = 10000.0
SSM_HEADDIM = 64
SSM_GROUPS = 4
SSM_STATE = 128
SSM_CONV = 4
N_EXPERTS = 8

LANES = 128
SUBLANES = 8
VMEM_LIMIT = 56 * 1024 * 1024

ROW_TILE = 512
FFN_ROW_TILE = 512
FFN_COL_TILE = 256
ATTN_TILE = 512
ATTN_STRIP = 32
ATTN_HEADS_PER_STEP = 16
ATTN_UNROLL_HEADS = 4
SSD_BLOCK = 512
ADA_COL_TILE = 1536
DISPATCH_TILE = 256
MOE_ROW_TILE = 512
DMA_UNROLL = 8

HEAD_PAD = LANES
V_PAD = LANES
ROPE_OFF = MLA_NOPE
NEG_BIG = -1e30
LOG2E = 1.4426950408889634


def _cparams(sem):
    return pltpu.CompilerParams(dimension_semantics=sem, vmem_limit_bytes=VMEM_LIMIT)


def _rmsnorm_mod(x, g, sh, sc):
    y = x * lax.rsqrt(jnp.mean(x * x, axis=-1, keepdims=True) + EPS) * g
    return y * (1.0 + sc) + sh


def _silu(v):
    hv = 0.5 * v
    return hv + hv * jnp.tanh(hv)


def _split3(v):
    hi = v.astype(BF16)
    r1 = v - hi.astype(F32)
    mid = r1.astype(BF16)
    lo = (r1 - mid.astype(F32)).astype(BF16)
    return hi, mid, lo


def _pack3(v, width):
    hi, mid, lo = (p.astype(F32) for p in _split3(v))
    return (hi + pltpu.roll(mid, width, axis=1) + pltpu.roll(lo, 2 * width, axis=1)).astype(BF16)


def _unpack3(r, width, real):
    return jnp.where(real, r + pltpu.roll(r, LANES - width, axis=1) + pltpu.roll(r, LANES - 2 * width, axis=1), 0.0)


def _dot(a, b):
    return jnp.dot(a, b, preferred_element_type=F32)


def _dot_nt(a, b):
    return lax.dot_general(a, b, (((1,), (1,)), ((), ())), preferred_element_type=F32)


def _adaln_kernel(c_ref, w_ref, b_ref, o_ref):
    c = c_ref[...]
    cond = _silu(c).astype(BF16)
    o_ref[0] = _dot(cond, w_ref[0].astype(BF16)) + b_ref[0]


def _adaln(c_pad, ada_w, ada_b):
    depth, d, n = ada_w.shape
    bp = c_pad.shape[0]
    tn = min(ADA_COL_TILE, n)
    return pl.pallas_call(
        _adaln_kernel,
        grid=(depth, n // tn),
        in_specs=[
            pl.BlockSpec((bp, d), lambda l, j: (0, 0)),
            pl.BlockSpec((1, d, tn), lambda l, j: (l, 0, j)),
            pl.BlockSpec((1, 1, tn), lambda l, j: (l, 0, j)),
        ],
        out_specs=pl.BlockSpec((1, bp, tn), lambda l, j: (l, 0, j)),
        out_shape=jax.ShapeDtypeStruct((depth, bp, n), F32),
        compiler_params=_cparams(("parallel", "parallel")),
        name="adaln",
    )(c_pad, ada_w, ada_b.reshape(depth, 1, n))


def _mla_in_kernel(x_ref, g_ref, sh_ref, sc_ref, win_ref, qn_ref, kvn_ref, wuq_ref, wukv_ref,
                   cq_ref, sq_ref, ck_ref, sk_ref, q_out, k_out, v_out):
    h = _rmsnorm_mod(x_ref[...], g_ref[...], sh_ref[0], sc_ref[0]).astype(BF16)
    proj = _dot(h, win_ref[...])
    cq = proj[:, :MLA_Q_LORA]
    cq = (cq * lax.rsqrt(jnp.mean(cq * cq, axis=-1, keepdims=True) + EPS) * qn_ref[...]).astype(BF16)
    ckv = proj[:, MLA_Q_LORA:MLA_Q_LORA + MLA_KV_LORA]
    ckv = (ckv * lax.rsqrt(jnp.mean(ckv * ckv, axis=-1, keepdims=True) + EPS) * kvn_ref[...]).astype(BF16)
    r0 = MLA_Q_LORA + MLA_KV_LORA
    kr = proj[:, r0:r0 + HEAD_PAD] * ck_ref[...] + proj[:, r0 + HEAD_PAD:r0 + 2 * HEAD_PAD] * sk_ref[...]
    nq = MLA_HEADS * HEAD_PAD
    qa = _dot(cq, wuq_ref[:, :nq])
    qb = _dot(cq, wuq_ref[:, nq:])
    kv = _dot(ckv, wukv_ref[...])
    cqt = cq_ref[...]
    sqt = sq_ref[...]
    for hd in range(MLA_HEADS):
        sl = slice(hd * HEAD_PAD, (hd + 1) * HEAD_PAD)
        q_out[:, sl] = (qa[:, sl] * cqt + qb[:, sl] * sqt).astype(BF16)
        k_out[:, sl] = (kv[:, sl] + kr).astype(BF16)
    ones_half = jnp.where(lax.broadcasted_iota(jnp.int32, (1, V_PAD), 1) >= MLA_V, 1.0, 0.0)
    for hd in range(MLA_HEADS):
        sl = slice(hd * V_PAD, (hd + 1) * V_PAD)
        v_out[:, sl] = (kv[:, nq + hd * V_PAD:nq + (hd + 1) * V_PAD] + ones_half).astype(BF16)


def _mla_in(x2, g, sh, sc, win, qn, kvn, wuq, wukv, cq, sq, ck, sk, seq):
    t, d = x2.shape
    tm = min(ROW_TILE, seq)
    tpb = seq // tm
    nq = MLA_HEADS * HEAD_PAD
    nv = MLA_HEADS * V_PAD
    row = lambda i: (i, 0)
    fixed = lambda i: (0, 0)
    per_b = lambda i: (i // tpb, 0, 0)
    return pl.pallas_call(
        _mla_in_kernel,
        grid=(t // tm,),
        in_specs=[
            pl.BlockSpec((tm, d), row),
            pl.BlockSpec((1, d), fixed),
            pl.BlockSpec((1, 1, d), per_b),
            pl.BlockSpec((1, 1, d), per_b),
            pl.BlockSpec(win.shape, fixed),
            pl.BlockSpec(qn.shape, fixed),
            pl.BlockSpec(kvn.shape, fixed),
            pl.BlockSpec(wuq.shape, fixed),
            pl.BlockSpec(wukv.shape, fixed),
            pl.BlockSpec((tm, HEAD_PAD), row),
            pl.BlockSpec((tm, HEAD_PAD), row),
            pl.BlockSpec((tm, HEAD_PAD), row),
            pl.BlockSpec((tm, HEAD_PAD), row),
        ],
        out_specs=[
            pl.BlockSpec((tm, nq), row),
            pl.BlockSpec((tm, nq), row),
            pl.BlockSpec((tm, nv), row),
        ],
        out_shape=[
            jax.ShapeDtypeStruct((t, nq), BF16),
            jax.ShapeDtypeStruct((t, nq), BF16),
            jax.ShapeDtypeStruct((t, nv), BF16),
        ],
        compiler_params=_cparams(("parallel",)),
        name="mla_in",
    )(x2, g, sh, sc, win, qn, kvn, wuq, wukv, cq, sq, ck, sk)


def _attn_kernel(qi_tab, kj_tab, q_ref, k_ref, v_ref, o_ref, m_sc, acc_sc, s_sc, p_sc, a_sc):
    t = pl.program_id(2)
    qi = qi_tab[t]
    kj = kj_tab[t]
    tq = q_ref.shape[1]
    tk = k_ref.shape[1]
    nlt = tk // LANES
    nstrip = tq // ATTN_STRIP
    nun = s_sc.shape[0]
    ngroup = q_ref.shape[2] // (nun * HEAD_PAD)

    @pl.when(kj == 0)
    def _():
        m_sc[...] = jnp.full(m_sc.shape, NEG_BIG, F32)
        acc_sc[...] = jnp.zeros(acc_sc.shape, F32)

    def visible_tiles(i, diagonal):
        if not diagonal:
            return nlt, 0
        rc = (i * ATTN_STRIP) // CHUNK
        per_tile = LANES // CHUNK
        return (rc + 1) // per_tile, ((rc + 1) % per_tile) * CHUNK

    def qk(g, u):
        c0 = pl.multiple_of((g * nun + u) * HEAD_PAD, HEAD_PAD)
        s_sc[u] = _dot_nt(q_ref[0, :, pl.ds(c0, HEAD_PAD)], k_ref[0, :, pl.ds(c0, HEAD_PAD)])

    def softmax(g, u, diagonal):
        hd = g * nun + u
        lane = lax.broadcasted_iota(jnp.int32, (ATTN_STRIP, LANES), 1)
        for i in range(nstrip):
            r = slice(i * ATTN_STRIP, (i + 1) * ATTN_STRIP)
            full, part = visible_tiles(i, diagonal)
            mx = None
            for j in range(full + (1 if part else 0)):
                sj = s_sc[u, r, j * LANES:(j + 1) * LANES]
                if j == full:
                    sj = jnp.where(lane < part, sj, NEG_BIG)
                mx = sj if mx is None else jnp.maximum(mx, sj)
            m_prev = m_sc[hd, r, :]
            m_new = jnp.maximum(m_prev, jnp.max(mx, axis=-1, keepdims=True))
            a_sc[u, r, :] = jnp.exp2(m_prev - m_new)
            m_sc[hd, r, :] = m_new
        for i in range(nstrip):
            r = slice(i * ATTN_STRIP, (i + 1) * ATTN_STRIP)
            full, part = visible_tiles(i, diagonal)
            m_new = m_sc[hd, r, :]
            for j in range(nlt):
                c = slice(j * LANES, (j + 1) * LANES)
                if j < full or (j == full and part):
                    p = jnp.exp2(s_sc[u, r, c] - m_new)
                    if j == full:
                        p = jnp.where(lane < part, p, 0.0)
                    p_sc[u, r, c] = p.astype(BF16)
                else:
                    p_sc[u, r, c] = jnp.zeros((ATTN_STRIP, LANES), BF16)

    def pv(g, u):
        hd = g * nun + u
        c0 = pl.multiple_of(hd * V_PAD, V_PAD)
        acc_sc[hd] = acc_sc[hd] * a_sc[u] + _dot(p_sc[u], v_ref[0, :, pl.ds(c0, V_PAD)])

    def group_step(g, diagonal):
        for stage in range(nun + 2):
            if stage < nun:
                qk(g, stage)
            if 0 <= stage - 1 < nun:
                softmax(g, stage - 1, diagonal)
            if 0 <= stage - 2 < nun:
                pv(g, stage - 2)

    @pl.when(kj < qi)
    def _():
        def body(g, c):
            group_step(g, False)
            return c
        lax.fori_loop(0, ngroup, body, 0)

    @pl.when(kj == qi)
    def _():
        left = lax.broadcasted_iota(jnp.int32, (tq, LANES), 1) < MLA_V

        def body(g, c):
            group_step(g, True)
            for u in range(0, nun, 2):
                outs = []
                for hd in (g * nun + u, g * nun + u + 1):
                    acc = acc_sc[hd]
                    outs.append(acc / pltpu.roll(acc, MLA_V, axis=1))
                o0 = pl.multiple_of((g * nun + u) * MLA_V, 2 * MLA_V)
                o_ref[0, :, pl.ds(o0, 2 * MLA_V)] = jnp.where(
                    left, outs[0], pltpu.roll(outs[1], MLA_V, axis=1)).astype(o_ref.dtype)
            return c
        lax.fori_loop(0, ngroup, body, 0)


def _attention(q, k, v, bsz, seq):
    tile = min(ATTN_TILE, seq)
    hps = ATTN_HEADS_PER_STEP
    nun = ATTN_UNROLL_HEADS
    qi_list, kj_list = [], []
    for a in range(seq // tile):
        for b in range(a + 1):
            qi_list.append(a)
            kj_list.append(b)
    qi_tab = jnp.asarray(qi_list, jnp.int32)
    kj_tab = jnp.asarray(kj_list, jnp.int32)
    q3 = q.reshape(bsz, seq, MLA_HEADS * HEAD_PAD)
    k3 = k.reshape(bsz, seq, MLA_HEADS * HEAD_PAD)
    v3 = v.reshape(bsz, seq, MLA_HEADS * V_PAD)
    grid_spec = pltpu.PrefetchScalarGridSpec(
        num_scalar_prefetch=2,
        grid=(bsz, MLA_HEADS // hps, len(qi_list)),
        in_specs=[
            pl.BlockSpec((1, tile, hps * HEAD_PAD), lambda b, hg, t, qt, kt: (b, qt[t], hg)),
            pl.BlockSpec((1, tile, hps * HEAD_PAD), lambda b, hg, t, qt, kt: (b, kt[t], hg)),
            pl.BlockSpec((1, tile, hps * V_PAD), lambda b, hg, t, qt, kt: (b, kt[t], hg)),
        ],
        out_specs=pl.BlockSpec((1, tile, hps * MLA_V), lambda b, hg, t, qt, kt: (b, qt[t], hg)),
        scratch_shapes=[
            pltpu.VMEM((hps, tile, LANES), F32),
            pltpu.VMEM((hps, tile, V_PAD), F32),
            pltpu.VMEM((nun, tile, tile), F32),
            pltpu.VMEM((nun, tile, tile), BF16),
            pltpu.VMEM((nun, tile, LANES), F32),
        ],
    )
    o = pl.pallas_call(
        _attn_kernel,
        grid_spec=grid_spec,
        out_shape=jax.ShapeDtypeStruct((bsz, seq, MLA_HEADS * MLA_V), BF16),
        compiler_params=_cparams(("parallel", "parallel", "arbitrary")),
        name="mla_attention",
    )(qi_tab, kj_tab, q3, k3, v3)
    return o.reshape(bsz * seq, MLA_HEADS * MLA_V)


def _proj_res_kernel(a_ref, w_ref, x_ref, gate_ref, o_ref):
    o_ref[...] = x_ref[...] + gate_ref[0] * _dot(a_ref[...], w_ref[...])


def _proj_residual(a, w, x2, gate, seq):
    t, d = x2.shape
    kdim = a.shape[1]
    tm = min(ROW_TILE, seq)
    tpb = seq // tm
    return pl.pallas_call(
        _proj_res_kernel,
        grid=(t // tm,),
        in_specs=[
            pl.BlockSpec((tm, kdim), lambda i: (i, 0)),
            pl.BlockSpec((kdim, d), lambda i: (0, 0)),
            pl.BlockSpec((tm, d), lambda i: (i, 0)),
            pl.BlockSpec((1, 1, d), lambda i: (i // tpb, 0, 0)),
        ],
        out_specs=pl.BlockSpec((tm, d), lambda i: (i, 0)),
        out_shape=jax.ShapeDtypeStruct((t, d), F32),
        compiler_params=_cparams(("parallel",)),
        name="proj_residual",
    )(a, w, x2, gate)


def _swiglu_chunks(h_sc, wg_ref, wu_ref, wd_ref, acc_sc, lead):
    dff = wg_ref.shape[-1]
    tf = min(FFN_COL_TILE, dff)

    def chunk(c, carry):
        f0 = pl.multiple_of(c * tf, tf)
        h = h_sc[...]
        gt = _dot(h, wg_ref[lead + (slice(None), pl.ds(f0, tf))])
        up = _dot(h, wu_ref[lead + (slice(None), pl.ds(f0, tf))])
        acc_sc[...] += _dot((_silu(gt) * up).astype(BF16), wd_ref[lead + (pl.ds(f0, tf), slice(None))])
        return carry

    acc_sc[...] = jnp.zeros(acc_sc.shape, F32)
    lax.fori_loop(0, dff // tf, chunk, 0, unroll=True)


def _ffn_kernel(x_ref, g_ref, sh_ref, sc_ref, gate_ref, wg_ref, wu_ref, wd_ref, o_ref, h_sc, acc_sc):
    h_sc[...] = _rmsnorm_mod(x_ref[...], g_ref[...], sh_ref[0], sc_ref[0]).astype(BF16)
    _swiglu_chunks(h_sc, wg_ref, wu_ref, wd_ref, acc_sc, ())
    o_ref[...] = x_ref[...] + gate_ref[0] * acc_sc[...]


def _ffn(x2, g, sh, sc, gate, wg, wu, wd, seq):
    t, d = x2.shape
    tm = min(FFN_ROW_TILE, seq)
    tpb = seq // tm
    per_b = lambda i: (i // tpb, 0, 0)
    fixed = lambda i: (0, 0)
    return pl.pallas_call(
        _ffn_kernel,
        grid=(t // tm,),
        in_specs=[
            pl.BlockSpec((tm, d), lambda i: (i, 0)),
            pl.BlockSpec((1, d), fixed),
            pl.BlockSpec((1, 1, d), per_b),
            pl.BlockSpec((1, 1, d), per_b),
            pl.BlockSpec((1, 1, d), per_b),
            pl.BlockSpec(wg.shape, fixed),
            pl.BlockSpec(wu.shape, fixed),
            pl.BlockSpec(wd.shape, fixed),
        ],
        out_specs=pl.BlockSpec((tm, d), lambda i: (i, 0)),
        out_shape=jax.ShapeDtypeStruct((t, d), F32),
        scratch_shapes=[pltpu.VMEM((tm, d), BF16), pltpu.VMEM((tm, d), F32)],
        compiler_params=_cparams(("parallel",)),
        name="swiglu",
    )(x2, g, sh, sc, gate, wg, wu, wd)


INFO_E1, INFO_E2, INFO_W1, INFO_W2, INFO_RANK1, INFO_RANK2 = range(6)


def _router_kernel(x_ref, g_ref, sh_ref, sc_ref, whi_ref, wlo_ref, tri_ref, info_ref, cnt_ref, carry_sc):
    @pl.when(pl.program_id(0) == 0)
    def _():
        carry_sc[...] = jnp.zeros(carry_sc.shape, F32)

    h = _rmsnorm_mod(x_ref[...], g_ref[...], sh_ref[0], sc_ref[0])
    h_hi = h.astype(BF16)
    h_lo = (h - h_hi.astype(F32)).astype(BF16)
    whi = whi_ref[...]
    logits = _dot(h_hi, whi) + _dot(h_lo, whi) + _dot(h_hi, wlo_ref[...])
    lane = lax.broadcasted_iota(jnp.int32, logits.shape, 1)
    neg = jnp.float32(-jnp.inf)
    lg = jnp.where(lane < N_EXPERTS, logits, neg)
    m1 = jnp.max(lg, axis=-1, keepdims=True)
    i1 = jnp.min(jnp.where(lg == m1, lane, LANES), axis=-1, keepdims=True)
    lg2 = jnp.where(lane == i1, neg, lg)
    m2 = jnp.max(lg2, axis=-1, keepdims=True)
    i2 = jnp.min(jnp.where(lg2 == m2, lane, LANES), axis=-1, keepdims=True)
    e2 = jnp.exp(m2 - m1)
    w1 = 1.0 / (1.0 + e2)
    w2 = e2 / (1.0 + e2)
    sel1 = lane == i1
    sel2 = lane == i2
    onehot = jnp.where(sel1 | sel2, 1.0, 0.0)
    before = carry_sc[0:1, :] + _dot(tri_ref[...], onehot.astype(BF16))
    rank1 = jnp.sum(jnp.where(sel1, before, 0.0), axis=-1, keepdims=True)
    rank2 = jnp.sum(jnp.where(sel2, before, 0.0), axis=-1, keepdims=True)
    carry_sc[0:1, :] = carry_sc[0:1, :] + jnp.sum(onehot, axis=0, keepdims=True)
    info = jnp.zeros(logits.shape, F32)
    for k, val in ((INFO_E1, i1.astype(F32)), (INFO_E2, i2.astype(F32)), (INFO_W1, w1), (INFO_W2, w2),
                   (INFO_RANK1, rank1), (INFO_RANK2, rank2)):
        info = jnp.where(lane == k, val, info)
    info_ref[...] = info
    cnt_ref[...] = carry_sc[...]


def _router(x2, g, sh, sc, whi, wlo, seq):
    t, d = x2.shape
    tm = min(ROW_TILE, seq)
    tpb = seq // tm
    per_b = lambda i: (i // tpb, 0, 0)
    r = jnp.arange(tm)
    tri = (r[None, :] < r[:, None]).astype(BF16)
    return pl.pallas_call(
        _router_kernel,
        grid=(t // tm,),
        in_specs=[
            pl.BlockSpec((tm, d), lambda i: (i, 0)),
            pl.BlockSpec((1, d), lambda i: (0, 0)),
            pl.BlockSpec((1, 1, d), per_b),
            pl.BlockSpec((1, 1, d), per_b),
            pl.BlockSpec((d, LANES), lambda i: (0, 0)),
            pl.BlockSpec((d, LANES), lambda i: (0, 0)),
            pl.BlockSpec((tm, tm), lambda i: (0, 0)),
        ],
        out_specs=[
            pl.BlockSpec((tm, LANES), lambda i: (i, 0)),
            pl.BlockSpec((SUBLANES, LANES), lambda i: (0, 0)),
        ],
        out_shape=[
            jax.ShapeDtypeStruct((t, LANES), F32),
            jax.ShapeDtypeStruct((SUBLANES, LANES), F32),
        ],
        scratch_shapes=[pltpu.VMEM((SUBLANES, LANES), F32)],
        compiler_params=_cparams(("arbitrary",)),
        name="moe_router",
    )(x2, g, sh, sc, whi, wlo, tri)


def _row_copy(src_ref, src_row, dst_ref, dst_row, sem):
    return pltpu.make_async_copy(src_ref.at[pl.ds(src_row, 1), :], dst_ref.at[pl.ds(dst_row, 1), :], sem)


def _load_row_indices(r1_ref, r2_ref, idx_sm, isem):
    c1 = pltpu.make_async_copy(r1_ref.at[0], idx_sm.at[0], isem.at[0])
    c2 = pltpu.make_async_copy(r2_ref.at[0], idx_sm.at[1], isem.at[1])
    c1.start()
    c2.start()
    return c1, c2


def _dispatch_kernel(zt_ref, x_ref, g_ref, sh_ref, sc_ref, r1_ref, r2_ref, xs_ref, h_sc, z_sc, idx_sm, sem, isem,
                     zsem):
    tm = x_ref.shape[0]
    ztm = z_sc.shape[0]

    @pl.when(pl.program_id(0) == 0)
    def _():
        z_sc[...] = jnp.zeros(z_sc.shape, F32)
        for i in range(zt_ref.shape[0]):
            cp = pltpu.make_async_copy(z_sc, xs_ref.at[pl.ds(pl.multiple_of(zt_ref[i] * ztm, ztm), ztm), :], zsem)
            cp.start()
            cp.wait()

    c1, c2 = _load_row_indices(r1_ref, r2_ref, idx_sm, isem)
    h_sc[...] = _rmsnorm_mod(x_ref[...], g_ref[...], sh_ref[0], sc_ref[0])
    c1.wait()
    c2.wait()

    def copies(m):
        return (_row_copy(h_sc, m, xs_ref, idx_sm[0, 0, m], sem), _row_copy(h_sc, m, xs_ref, idx_sm[1, 0, m], sem))

    def issue(m, c):
        for queue, cp in enumerate(copies(m)):
            cp.start(priority=queue)
        return c

    def drain(m, c):
        for cp in copies(m):
            cp.wait()
        return c

    lax.fori_loop(0, tm, issue, 0, unroll=DMA_UNROLL)
    lax.fori_loop(0, tm, drain, 0, unroll=DMA_UNROLL)


def _dispatch(x2, g, sh, sc, r1, r2, zero_tiles, rows_pad, seq):
    t, d = x2.shape
    tm = min(DISPATCH_TILE, seq)
    tpb = seq // tm
    per_b = lambda i, zt: (i // tpb, 0, 0)
    idx_spec = pl.BlockSpec((1, 1, tm), lambda i, zt: (i, 0, 0))
    grid_spec = pltpu.PrefetchScalarGridSpec(
        num_scalar_prefetch=1,
        grid=(t // tm,),
        in_specs=[
            pl.BlockSpec((tm, d), lambda i, zt: (i, 0)),
            pl.BlockSpec((1, d), lambda i, zt: (0, 0)),
            pl.BlockSpec((1, 1, d), per_b),
            pl.BlockSpec((1, 1, d), per_b),
            idx_spec,
            idx_spec,
        ],
        out_specs=pl.BlockSpec(memory_space=pl.ANY),
        scratch_shapes=[
            pltpu.VMEM((tm, d), F32),
            pltpu.VMEM((MOE_ROW_TILE, d), F32),
            pltpu.SMEM((2, 1, tm), jnp.int32),
            pltpu.SemaphoreType.DMA,
            pltpu.SemaphoreType.DMA((2,)),
            pltpu.SemaphoreType.DMA,
        ],
    )
    return pl.pallas_call(
        _dispatch_kernel,
        grid_spec=grid_spec,
        out_shape=jax.ShapeDtypeStruct((rows_pad, d), F32),
        compiler_params=_cparams(("arbitrary",)),
        name="moe_dispatch",
    )(zero_tiles, x2, g, sh, sc, r1.reshape(t // tm, 1, tm), r2.reshape(t // tm, 1, tm))


def _expert_ffn_kernel(te_ref, nu_ref, x_ref, wg_ref, wu_ref, wd_ref, o_ref, h_sc, acc_sc):
    del te_ref

    @pl.when(pl.program_id(0) < nu_ref[0])
    def _():
        h_sc[...] = x_ref[...].astype(BF16)
        _swiglu_chunks(h_sc, wg_ref, wu_ref, wd_ref, acc_sc, (0,))
        o_ref[...] = acc_sc[...]

    @pl.when(pl.program_id(0) >= nu_ref[0])
    def _():
        o_ref[...] = jnp.zeros(o_ref.shape, F32)


def _expert_ffn(xs, tile_expert, n_used, wg, wu, wd):
    rows_pad, d = xs.shape
    dff = wg.shape[2]
    tm = MOE_ROW_TILE
    per_expert = lambda i, te, nu: (te[i], 0, 0)
    grid_spec = pltpu.PrefetchScalarGridSpec(
        num_scalar_prefetch=2,
        grid=(rows_pad // tm,),
        in_specs=[
            pl.BlockSpec((tm, d), lambda i, te, nu: (jnp.minimum(i, nu[0] - 1), 0)),
            pl.BlockSpec((1, d, dff), per_expert),
            pl.BlockSpec((1, d, dff), per_expert),
            pl.BlockSpec((1, dff, d), per_expert),
        ],
        out_specs=pl.BlockSpec((tm, d), lambda i, te, nu: (i, 0)),
        scratch_shapes=[pltpu.VMEM((tm, d), BF16), pltpu.VMEM((tm, d), F32)],
    )
    return pl.pallas_call(
        _expert_ffn_kernel,
        grid_spec=grid_spec,
        out_shape=jax.ShapeDtypeStruct((rows_pad, d), F32),
        compiler_params=_cparams(("arbitrary",)),
        name="moe_expert_ffn",
    )(tile_expert, n_used, xs, wg, wu, wd)


def _combine_kernel(x_ref, gate_ref, info_ref, r1_ref, r2_ref, fg_ref, ys_ref, o_ref, y1_sc, y2_sc, idx_sm, sem,
                    isem, *, final_norm):
    tm = x_ref.shape[0]
    c1, c2 = _load_row_indices(r1_ref, r2_ref, idx_sm, isem)
    c1.wait()
    c2.wait()

    def copies(m):
        return (_row_copy(ys_ref, idx_sm[0, 0, m], y1_sc, m, sem), _row_copy(ys_ref, idx_sm[1, 0, m], y2_sc, m, sem))

    def issue(m, c):
        for queue, cp in enumerate(copies(m)):
            cp.start(priority=queue)
        return c

    def drain(m, c):
        for cp in copies(m):
            cp.wait()
        return c

    lax.fori_loop(0, tm, issue, 0, unroll=DMA_UNROLL)
    lax.fori_loop(0, tm, drain, 0, unroll=DMA_UNROLL)
    info = info_ref[...]
    w1 = info[:, INFO_W1:INFO_W1 + 1]
    w2 = info[:, INFO_W2:INFO_W2 + 1]
    out = x_ref[...] + gate_ref[0] * (w1 * y1_sc[...] + w2 * y2_sc[...])
    if final_norm:
        out = out * lax.rsqrt(jnp.mean(out * out, axis=-1, keepdims=True) + EPS) * fg_ref[...]
    o_ref[...] = out


def _combine(x2, gate, info, r1, r2, ys, final_gain, seq, final_norm):
    t, d = x2.shape
    tm = min(DISPATCH_TILE, seq)
    tpb = seq // tm
    idx_spec = pl.BlockSpec((1, 1, tm), lambda i: (i, 0, 0))
    return pl.pallas_call(
        functools.partial(_combine_kernel, final_norm=final_norm),
        grid=(t // tm,),
        in_specs=[
            pl.BlockSpec((tm, d), lambda i: (i, 0)),
            pl.BlockSpec((1, 1, d), lambda i: (i // tpb, 0, 0)),
            pl.BlockSpec((tm, LANES), lambda i: (i, 0)),
            idx_spec,
            idx_spec,
            pl.BlockSpec((1, d), lambda i: (0, 0)),
            pl.BlockSpec(memory_space=pl.ANY),
        ],
        out_specs=pl.BlockSpec((tm, d), lambda i: (i, 0)),
        out_shape=jax.ShapeDtypeStruct((t, d), F32),
        scratch_shapes=[
            pltpu.VMEM((tm, d), F32),
            pltpu.VMEM((tm, d), F32),
            pltpu.SMEM((2, 1, tm), jnp.int32),
            pltpu.SemaphoreType.DMA,
            pltpu.SemaphoreType.DMA((2,)),
        ],
        compiler_params=_cparams(("arbitrary",)),
        name="moe_combine",
    )(x2, gate, info, r1.reshape(t // tm, 1, tm), r2.reshape(t // tm, 1, tm), final_gain, ys)


def _moe_plan(info, counts, n_tokens):
    tm = MOE_ROW_TILE
    cnt = counts[0, :N_EXPERTS].astype(jnp.int32)
    padded = ((cnt + tm - 1) // tm) * tm
    ends = jnp.cumsum(padded)
    starts = ends - padded
    e1 = info[:, INFO_E1].astype(jnp.int32)
    e2 = info[:, INFO_E2].astype(jnp.int32)
    r1 = starts[e1] + info[:, INFO_RANK1].astype(jnp.int32)
    r2 = starts[e2] + info[:, INFO_RANK2].astype(jnp.int32)
    n_tiles = (2 * n_tokens) // tm + N_EXPERTS
    n_used = (ends[-1] // tm).reshape(1)
    tile_start = jnp.minimum(jnp.arange(n_tiles, dtype=jnp.int32), n_used - 1) * tm
    tile_expert = jnp.sum(tile_start[:, None] >= ends[None, :], axis=1).astype(jnp.int32)
    group_last = jnp.maximum(ends // tm - 1, 0)
    tail = jnp.minimum(n_used + jnp.arange(N_EXPERTS, dtype=jnp.int32), n_tiles - 1)
    zero_tiles = jnp.concatenate([group_last, tail]).astype(jnp.int32)
    return r1, r2, tile_expert, n_used.astype(jnp.int32), zero_tiles, n_tiles * tm


def _ssm_in_kernel(x_ref, g_ref, sh_ref, sc_ref, wz_ref, wx_ref, wdt_ref, z_out, xbc_out, dt_out):
    h = _rmsnorm_mod(x_ref[...], g_ref[...], sh_ref[0], sc_ref[0]).astype(BF16)
    z_out[...] = _dot(h, wz_ref[...]).astype(BF16)
    xbc_out[...] = _dot(h, wx_ref[...]).astype(BF16)
    dt_out[...] = _dot(h, wdt_ref[...])


def _ssm_in(x2, g, sh, sc, wz, wx, wdt, seq):
    t, d = x2.shape
    tm = min(ROW_TILE, seq)
    tpb = seq // tm
    per_b = lambda i: (i // tpb, 0, 0)
    fixed = lambda i: (0, 0)
    row = lambda i: (i, 0)
    return pl.pallas_call(
        _ssm_in_kernel,
        grid=(t // tm,),
        in_specs=[
            pl.BlockSpec((tm, d), row),
            pl.BlockSpec((1, d), fixed),
            pl.BlockSpec((1, 1, d), per_b),
            pl.BlockSpec((1, 1, d), per_b),
            pl.BlockSpec(wz.shape, fixed),
            pl.BlockSpec(wx.shape, fixed),
            pl.BlockSpec(wdt.shape, fixed),
        ],
        out_specs=[
            pl.BlockSpec((tm, wz.shape[1]), row),
            pl.BlockSpec((tm, wx.shape[1]), row),
            pl.BlockSpec((tm, LANES), row),
        ],
        out_shape=[
            jax.ShapeDtypeStruct((t, wz.shape[1]), BF16),
            jax.ShapeDtypeStruct((t, wx.shape[1]), BF16),
            jax.ShapeDtypeStruct((t, LANES), F32),
        ],
        compiler_params=_cparams(("parallel",)),
        name="ssm_in",
    )(x2, g, sh, sc, wz, wx, wdt)


def _ssd_kernel(z_ref, xbc_ref, dt_ref, cw_ref, cb_ref, dtb_ref, alog_ref, dskip_ref, ng_ref,
                tri_ref, exp_ref, eye_ref, o_ref, ubuf, acs_sc, dtx_sc, state_sc, *, d_inner):
    blk = z_ref.shape[0]
    gw = d_inner // SSM_GROUPS
    gn = SSM_GROUPS * SSM_STATE
    pair = 2 * SSM_HEADDIM

    @pl.when(pl.program_id(1) == 0)
    def _():
        ubuf[0:SUBLANES, :] = jnp.zeros((SUBLANES, ubuf.shape[1]), F32)
        state_sc[...] = jnp.zeros(state_sc.shape, F32)

    ubuf[SUBLANES:SUBLANES + blk, :] = xbc_ref[...].astype(F32)

    lane = lax.broadcasted_iota(jnp.int32, (1, LANES), 1)
    heads = d_inner // SSM_HEADDIM
    real = lane < heads
    a = jnp.where(real, -jnp.exp(alog_ref[...]), 0.0)
    v = dt_ref[...] + dtb_ref[...]
    dt = jnp.where(real, jnp.maximum(v, 0.0) + jnp.log(1.0 + jnp.exp(-jnp.abs(v))), 0.0)
    acs = _unpack3(_dot(tri_ref[...], _pack3(dt * a, heads)), heads, real)
    expand = exp_ref[...]
    acs_sc[...] = _dot(_pack3(acs, heads), expand)
    dtx_sc[...] = _dot(_pack3(dt, heads), expand)

    cw = cw_ref[...]
    cbias = cb_ref[...]
    dskip = dskip_ref[...]
    ngain = ng_ref[...]
    eye = eye_ref[...]
    row_i = lax.broadcasted_iota(jnp.int32, (CHUNK, pair), 0)
    col_i = lax.broadcasted_iota(jnp.int32, (CHUNK, pair), 1)
    col_s = jnp.where(col_i >= SSM_HEADDIM, col_i - SSM_HEADDIM, col_i)
    diag = row_i == col_s
    causal = col_s <= row_i
    first_head = col_i < SSM_HEADDIM

    def chunk(c, carry):
        r0 = pl.multiple_of(c * CHUNK, CHUNK)
        win = ubuf[pl.ds(r0, CHUNK + SUBLANES), :]
        u = cbias
        for j in range(SSM_CONV):
            lo = SUBLANES - (SSM_CONV - 1) + j
            u = u + cw[j:j + 1, :] * win[lo:lo + CHUNK, :]
        xbc = _silu(u)
        xs = xbc[:, :d_inner]
        bm = xbc[:, d_inner:d_inner + gn].astype(BF16)
        cm = xbc[:, d_inner + gn:].astype(BF16)
        ae = acs_sc[pl.ds(r0, CHUNK), :]
        xdt = xs * dtx_sc[pl.ds(r0, CHUNK), :]
        last = acs_sc[pl.ds(r0 + CHUNK - 1, 1), :]
        from_start = jnp.exp(ae)
        xdte = (xdt * jnp.exp(last - ae)).astype(BF16)
        chunk_decay = jnp.exp(last)
        zc = z_ref[pl.ds(r0, CHUNK), :].astype(F32)
        zgate = _silu(zc)
        for g in range(SSM_GROUPS):
            bg = bm[:, g * SSM_STATE:(g + 1) * SSM_STATE]
            cg = cm[:, g * SSM_STATE:(g + 1) * SSM_STATE]
            gs = slice(g * gw, (g + 1) * gw)
            cbcb = _dot_nt(cg, jnp.concatenate([bg, bg], axis=0))
            s_prev = state_sc[g]
            y_off = _dot(cg, s_prev.astype(BF16)) * from_start[:, gs]
            bg_t = _dot_nt(eye, bg).astype(BF16)
            state_sc[g] = s_prev * chunk_decay[:, gs] + _dot(bg_t, xdte[:, gs])
            ys = []
            for pr in range(gw // pair):
                ps = slice(g * gw + pr * pair, g * gw + (pr + 1) * pair)
                dcol = ae[:, ps]
                drow = jnp.sum(jnp.where(diag, dcol, 0.0), axis=0, keepdims=True)
                decay = jnp.where(causal, jnp.exp(jnp.where(causal, dcol - drow, 0.0)), 0.0)
                m = (cbcb * decay).astype(BF16)
                xp = xdt[:, ps]
                bd = jnp.concatenate([jnp.where(first_head, xp, 0.0), jnp.where(first_head, 0.0, xp)],
                                     axis=0).astype(BF16)
                ys.append(_dot(m, bd) + y_off[:, pr * pair:(pr + 1) * pair])
            yg = (jnp.concatenate(ys, axis=1) + dskip[:, gs] * xs[:, gs]) * zgate[:, gs]
            yn = yg * lax.rsqrt(jnp.mean(yg * yg, axis=-1, keepdims=True) + EPS) * ngain[:, gs]
            o_ref[pl.ds(r0, CHUNK), gs] = yn.astype(o_ref.dtype)
        return carry

    lax.fori_loop(0, blk // CHUNK, chunk, 0)
    ubuf[0:SUBLANES, :] = ubuf[blk:blk + SUBLANES, :]


def _ssd(z, xbc, dt, cw, cb, dtb, alog, dskip, ng, bsz, seq):
    d_inner = z.shape[1]
    cdim = xbc.shape[1]
    blk = min(SSD_BLOCK, seq)
    nblk = seq // blk
    r = jnp.arange(blk)
    tri = (((r[:, None] // CHUNK) == (r[None, :] // CHUNK)) & (r[None, :] <= r[:, None])).astype(BF16)
    heads = d_inner // SSM_HEADDIM
    assert 3 * heads <= LANES
    src = jnp.arange(LANES)[:, None]
    expand = ((src < 3 * heads) & (src % heads == jnp.arange(d_inner)[None, :] // SSM_HEADDIM)).astype(BF16)
    eye = jnp.eye(SSM_STATE, dtype=BF16)
    row = lambda b, i: (b * nblk + i, 0)
    fixed = lambda b, i: (0, 0)
    return pl.pallas_call(
        functools.partial(_ssd_kernel, d_inner=d_inner),
        grid=(bsz, nblk),
        in_specs=[
            pl.BlockSpec((blk, d_inner), row),
            pl.BlockSpec((blk, cdim), row),
            pl.BlockSpec((blk, LANES), row),
            pl.BlockSpec(cw.shape, fixed),
            pl.BlockSpec(cb.shape, fixed),
            pl.BlockSpec(dtb.shape, fixed),
            pl.BlockSpec(alog.shape, fixed),
            pl.BlockSpec(dskip.shape, fixed),
            pl.BlockSpec(ng.shape, fixed),
            pl.BlockSpec(tri.shape, fixed),
            pl.BlockSpec(expand.shape, fixed),
            pl.BlockSpec(eye.shape, fixed),
        ],
        out_specs=pl.BlockSpec((blk, d_inner), row),
        out_shape=jax.ShapeDtypeStruct((bsz * seq, d_inner), BF16),
        scratch_shapes=[
            pltpu.VMEM((blk + SUBLANES, cdim), F32),
            pltpu.VMEM((blk, d_inner), F32),
            pltpu.VMEM((blk, d_inner), F32),
            pltpu.VMEM((SSM_GROUPS, SSM_STATE, d_inner // SSM_GROUPS), F32),
        ],
        compiler_params=_cparams(("parallel", "arbitrary")),
        name="ssd",
    )(z, xbc, dt, cw, cb, dtb, alog, dskip, ng, tri, expand, eye)


def _final_norm_kernel(x_ref, g_ref, o_ref):
    x = x_ref[...]
    o_ref[...] = x * lax.rsqrt(jnp.mean(x * x, axis=-1, keepdims=True) + EPS) * g_ref[...]


def _final_norm(x2, g, seq):
    t, d = x2.shape
    tm = min(FFN_ROW_TILE, seq)
    return pl.pallas_call(
        _final_norm_kernel,
        grid=(t // tm,),
        in_specs=[pl.BlockSpec((tm, d), lambda i: (i, 0)), pl.BlockSpec((1, d), lambda i: (0, 0))],
        out_specs=pl.BlockSpec((tm, d), lambda i: (i, 0)),
        out_shape=jax.ShapeDtypeStruct((t, d), F32),
        compiler_params=_cparams(("parallel",)),
        name="final_norm",
    )(x2, g)


def _swap_halves(w):
    half = w.shape[-1] // 2
    return jnp.concatenate([w[..., half:], w[..., :half]], axis=-1)


def _head_block(nope, rope):
    pad = jnp.zeros(rope.shape[:-1] + (HEAD_PAD - MLA_NOPE - MLA_ROPE,), rope.dtype)
    blk = jnp.concatenate([nope, rope, pad], axis=-1)
    return blk.reshape(blk.shape[:-2] + (blk.shape[-2] * HEAD_PAD,))


def _mla_weights(w_in, w_uq, w_ukv):
    d = w_in.shape[0]
    r0 = MLA_Q_LORA + MLA_KV_LORA
    w_kr = w_in[:, r0:]
    z_nope = jnp.zeros((d, 1, MLA_NOPE), w_in.dtype)
    win = jnp.concatenate([w_in[:, :r0], _head_block(z_nope, w_kr[:, None, :]),
                           _head_block(z_nope, _swap_halves(w_kr)[:, None, :])], axis=1)
    uq = w_uq.reshape(MLA_Q_LORA, MLA_HEADS, MLA_NOPE + MLA_ROPE)
    uq_nope, uq_rope = uq[..., :MLA_NOPE], uq[..., MLA_NOPE:]
    wuq = jnp.concatenate([_head_block(uq_nope, uq_rope),
                           _head_block(jnp.zeros_like(uq_nope), _swap_halves(uq_rope))], axis=1)
    ukv = w_ukv.reshape(MLA_KV_LORA, MLA_HEADS, MLA_NOPE + MLA_V)
    uk, uv = ukv[..., :MLA_NOPE], ukv[..., MLA_NOPE:]
    wk = _head_block(uk, jnp.zeros(uk.shape[:-1] + (MLA_ROPE,), uk.dtype))
    wv = jnp.concatenate([uv, jnp.zeros(uv.shape[:-1] + (V_PAD - MLA_V,), uv.dtype)], axis=-1)
    wukv = jnp.concatenate([wk, wv.reshape(MLA_KV_LORA, MLA_HEADS * V_PAD)], axis=1)
    return win.astype(BF16), wuq.astype(BF16), wukv.astype(BF16)


def _rope_tables(positions):
    inv_freq = ROPE_THETA ** (-jnp.arange(0, MLA_ROPE, 2, dtype=F32) / MLA_ROPE)
    ang = positions.astype(F32).reshape(-1)[:, None] * inv_freq
    cos, sin = jnp.cos(ang), jnp.sin(ang)
    t = cos.shape[0]
    lead = jnp.zeros((t, ROPE_OFF), F32)
    tail = jnp.zeros((t, HEAD_PAD - ROPE_OFF - MLA_ROPE), F32)
    scale = (MLA_NOPE + MLA_ROPE) ** -0.5 * LOG2E
    ck =jnp.concatenate([lead, cos, cos, tail], axis=1)
    sk = jnp.concatenate([lead, -sin, sin, tail], axis=1)
    cq = jnp.concatenate([lead + 1.0, cos, cos, tail], axis=1) * scale
    sq = sk * scale
    return cq, sq, ck, sk


def _pad_lanes(v, fill=0.0):
    return jnp.pad(v, [(0, 0)] * (v.ndim - 1) + [(0, LANES - v.shape[-1])], constant_values=fill)


def kernel(x, c, positions, ada_w, ada_b, norm_g, mla_w_in, mla_q_norm, mla_kv_norm, mla_w_uq, mla_w_ukv, mla_w_out, ssm_w_in, ssm_conv_w, ssm_conv_b, ssm_dt_bias, ssm_a_log, ssm_d, ssm_norm, ssm_w_out, ffn_w_gate, ffn_w_up, ffn_w_down, moe_w_router, moe_w_gate, moe_w_up, moe_w_down, final_norm):
    bsz, seq, d = x.shape
    depth = ada_w.shape[0]
    d_inner = ssm_norm.shape[1]
    cdim = ssm_conv_w.shape[2]
    t = bsz * seq
    x2 = x.reshape(t, d)

    c_pad = jnp.pad(c, ((0, SUBLANES - bsz), (0, 0)))
    mod = _adaln(c_pad, ada_w, ada_b)[:, :bsz].reshape(depth, bsz, 6, 1, d)
    cq, sq, ck, sk = _rope_tables(positions)

    for i in range(depth):
        j = i // 2
        sh1, sc1, g1, sh2, sc2, g2 = [mod[i, :, k] for k in range(6)]
        gain1 = norm_g[i, 0][None, :]
        gain2 = norm_g[i, 1][None, :]
        if i % 2 == 0:
            win, wuq, wukv = _mla_weights(mla_w_in[j], mla_w_uq[j], mla_w_ukv[j])
            q, k, v = _mla_in(x2, gain1, sh1, sc1, win, mla_q_norm[j][None, :], mla_kv_norm[j][None, :],
                              wuq, wukv, cq, sq, ck, sk, seq)
            o = _attention(q, k, v, bsz, seq)
            x2 = _proj_residual(o, mla_w_out[j].astype(BF16), x2, g1, seq)
            x2 = _ffn(x2, gain2, sh2, sc2, g2, ffn_w_gate[j].astype(BF16), ffn_w_up[j].astype(BF16),
                      ffn_w_down[j].astype(BF16), seq)
        else:
            w_in = ssm_w_in[j]
            wz = w_in[:, :d_inner].astype(BF16)
            wx = w_in[:, d_inner:d_inner + cdim].astype(BF16)
            wdt = _pad_lanes(w_in[:, d_inner + cdim:]).astype(BF16)
            z, xbc, dt = _ssm_in(x2, gain1, sh1, sc1, wz, wx, wdt, seq)
            yn = _ssd(z, xbc, dt, jnp.pad(ssm_conv_w[j], ((0, SUBLANES - SSM_CONV), (0, 0))),
                      ssm_conv_b[j][None, :], _pad_lanes(ssm_dt_bias[j][None, :]),
                      _pad_lanes(ssm_a_log[j][None, :]), jnp.repeat(ssm_d[j], SSM_HEADDIM)[None, :],
                      ssm_norm[j][None, :], bsz, seq)
            x2 = _proj_residual(yn, ssm_w_out[j].astype(BF16), x2, g1, seq)
            wr = _pad_lanes(moe_w_router[j])
            wr_hi = wr.astype(BF16)
            wr_lo = (wr - wr_hi.astype(F32)).astype(BF16)
            info, counts = _router(x2, gain2, sh2, sc2, wr_hi, wr_lo, seq)
            r1, r2, tile_expert, n_used, zero_tiles, rows_pad = _moe_plan(info, counts, t)
            xs = _dispatch(x2, gain2, sh2, sc2, r1, r2, zero_tiles, rows_pad, seq)
            ys = _expert_ffn(xs, tile_expert, n_used, moe_w_gate[j].astype(BF16), moe_w_up[j].astype(BF16),
                             moe_w_down[j].astype(BF16))
            x2 = _combine(x2, g2, info, r1, r2, ys, final_norm[None, :], seq, i == depth - 1)
    if depth % 2 == 1:
        x2 = _final_norm(x2, final_norm[None, :], seq)
    return x2.reshape(bsz, seq, d)
```

```python
import functools

import jax
import jax.numpy as jnp
from jax import lax
from jax.experimental import pallas as pl
from jax.experimental.pallas import tpu as pltpu

F32 = jnp.float32
BF16 = jnp.bfloat16

EPS = 1e-6
CHUNK = 64
MLA_HEADS = 16
MLA_NOPE = 64
MLA_ROPE = 32
MLA_V = 64
MLA_Q_LORA = 384
MLA_KV_LORA = 256
ROPE_THETA = 10000.0
SSM_HEADDIM = 64
SSM_GROUPS = 4
SSM_STATE = 128
SSM_CONV = 4
N_EXPERTS = 8

LANES = 128
SUBLANES = 8
VMEM_LIMIT = 56 * 1024 * 1024

ROW_TILE = 512
FFN_ROW_TILE = 512
FFN_COL_TILE = 256
ATTN_TILE = 512
ATTN_STRIP = 32
ATTN_HEADS_PER_STEP = 16
ATTN_UNROLL_HEADS = 4
SSD_BLOCK = 512
ADA_COL_TILE = 1536
DISPATCH_TILE = 256
MOE_ROW_TILE = 512
DMA_UNROLL = 8
SSD_UNROLL = 2
CONV_TAIL = 16

HEAD_PAD = LANES
V_PAD = LANES
ROPE_OFF = MLA_NOPE
NEG_BIG = -1e30
LOG2E = 1.4426950408889634


def _cparams(sem):
    return pltpu.CompilerParams(dimension_semantics=sem, vmem_limit_bytes=VMEM_LIMIT)


def _rmsnorm_mod(x, g, sh, sc):
    y = x * lax.rsqrt(jnp.mean(x * x, axis=-1, keepdims=True) + EPS) * g
    return y * (1.0 + sc) + sh


def _silu(v):
    hv = 0.5 * v
    return hv + hv * jnp.tanh(hv)


def _split3(v):
    hi = v.astype(BF16)
    r1 = v - hi.astype(F32)
    mid = r1.astype(BF16)
    lo = (r1 - mid.astype(F32)).astype(BF16)
    return hi, mid, lo


def _pack3(v, width):
    hi, mid, lo = (p.astype(F32) for p in _split3(v))
    return (hi + pltpu.roll(mid, width, axis=1) + pltpu.roll(lo, 2 * width, axis=1)).astype(BF16)


def _unpack3(r, width, real):
    return jnp.where(real, r + pltpu.roll(r, LANES - width, axis=1) + pltpu.roll(r, LANES - 2 * width, axis=1), 0.0)


def _dot(a, b):
    return jnp.dot(a, b, preferred_element_type=F32)


def _dot_nt(a, b):
    return lax.dot_general(a, b, (((1,), (1,)), ((), ())), preferred_element_type=F32)


def _adaln_kernel(c_ref, w_ref, b_ref, o_ref):
    c = c_ref[...]
    cond = _silu(c).astype(BF16)
    o_ref[0] = _dot(cond, w_ref[0].astype(BF16)) + b_ref[0]


def _adaln(c_pad, ada_w, ada_b):
    depth, d, n = ada_w.shape
    bp = c_pad.shape[0]
    tn = min(ADA_COL_TILE, n)
    return pl.pallas_call(
        _adaln_kernel,
        grid=(depth, n // tn),
        in_specs=[
            pl.BlockSpec((bp, d), lambda l, j: (0, 0)),
            pl.BlockSpec((1, d, tn), lambda l, j: (l, 0, j)),
            pl.BlockSpec((1, 1, tn), lambda l, j: (l, 0, j)),
        ],
        out_specs=pl.BlockSpec((1, bp, tn), lambda l, j: (l, 0, j)),
        out_shape=jax.ShapeDtypeStruct((depth, bp, n), F32),
        compiler_params=_cparams(("parallel", "parallel")),
        name="adaln",
    )(c_pad, ada_w, ada_b.reshape(depth, 1, n))


def _mla_in_kernel(x_ref, g_ref, sh_ref, sc_ref, win_ref, qn_ref, kvn_ref, wuq_ref, wukv_ref,
                   cq_ref, sq_ref, ck_ref, sk_ref, q_out, k_out, v_out):
    h = _rmsnorm_mod(x_ref[...], g_ref[...], sh_ref[0], sc_ref[0]).astype(BF16)
    proj = _dot(h, win_ref[...])
    cq = proj[:, :MLA_Q_LORA]
    cq = (cq * lax.rsqrt(jnp.mean(cq * cq, axis=-1, keepdims=True) + EPS) * qn_ref[...]).astype(BF16)
    ckv = proj[:, MLA_Q_LORA:MLA_Q_LORA + MLA_KV_LORA]
    ckv = (ckv * lax.rsqrt(jnp.mean(ckv * ckv, axis=-1, keepdims=True) + EPS) * kvn_ref[...]).astype(BF16)
    r0 = MLA_Q_LORA + MLA_KV_LORA
    kr = proj[:, r0:r0 + HEAD_PAD] * ck_ref[...] + proj[:, r0 + HEAD_PAD:r0 + 2 * HEAD_PAD] * sk_ref[...]
    nq = MLA_HEADS * HEAD_PAD
    qa = _dot(cq, wuq_ref[:, :nq])
    qb = _dot(cq, wuq_ref[:, nq:])
    kv = _dot(ckv, wukv_ref[...])
    cqt = cq_ref[...]
    sqt = sq_ref[...]
    for hd in range(MLA_HEADS):
        sl = slice(hd * HEAD_PAD, (hd + 1) * HEAD_PAD)
        q_out[:, sl] = (qa[:, sl] * cqt + qb[:, sl] * sqt).astype(BF16)
        k_out[:, sl] = (kv[:, sl] + kr).astype(BF16)
    ones_half = jnp.where(lax.broadcasted_iota(jnp.int32, (1, V_PAD), 1) >= MLA_V, 1.0, 0.0)
    for hd in range(MLA_HEADS):
        sl = slice(hd * V_PAD, (hd + 1) * V_PAD)
        v_out[:, sl] = (kv[:, nq + hd * V_PAD:nq + (hd + 1) * V_PAD] + ones_half).astype(BF16)


def _mla_in(x2, g, sh, sc, win, qn, kvn, wuq, wukv, cq, sq, ck, sk, seq):
    t, d = x2.shape
    tm = min(ROW_TILE, seq)
    tpb = seq // tm
    nq = MLA_HEADS * HEAD_PAD
    nv = MLA_HEADS * V_PAD
    row = lambda i: (i, 0)
    fixed = lambda i: (0, 0)
    per_b = lambda i: (i // tpb, 0, 0)
    return pl.pallas_call(
        _mla_in_kernel,
        grid=(t // tm,),
        in_specs=[
            pl.BlockSpec((tm, d), row),
            pl.BlockSpec((1, d), fixed),
            pl.BlockSpec((1, 1, d), per_b),
            pl.BlockSpec((1, 1, d), per_b),
            pl.BlockSpec(win.shape, fixed),
            pl.BlockSpec(qn.shape, fixed),
            pl.BlockSpec(kvn.shape, fixed),
            pl.BlockSpec(wuq.shape, fixed),
            pl.BlockSpec(wukv.shape, fixed),
            pl.BlockSpec((tm, HEAD_PAD), row),
            pl.BlockSpec((tm, HEAD_PAD), row),
            pl.BlockSpec((tm, HEAD_PAD), row),
            pl.BlockSpec((tm, HEAD_PAD), row),
        ],
        out_specs=[
            pl.BlockSpec((tm, nq), row),
            pl.BlockSpec((tm, nq), row),
            pl.BlockSpec((tm, nv), row),
        ],
        out_shape=[
            jax.ShapeDtypeStruct((t, nq), BF16),
            jax.ShapeDtypeStruct((t, nq), BF16),
            jax.ShapeDtypeStruct((t, nv), BF16),
        ],
        compiler_params=_cparams(("parallel",)),
        name="mla_in",
    )(x2, g, sh, sc, win, qn, kvn, wuq, wukv, cq, sq, ck, sk)


def _attn_kernel(qi_tab, kj_tab, q_ref, k_ref, v_ref, o_ref, m_sc, acc_sc, s_sc, p_sc, a_sc):
    t = pl.program_id(2)
    qi = qi_tab[t]
    kj = kj_tab[t]
    tq = q_ref.shape[1]
    tk = k_ref.shape[1]
    nlt = tk // LANES
    nstrip = tq // ATTN_STRIP
    nun = s_sc.shape[0]
    ngroup = q_ref.shape[2] // (nun * HEAD_PAD)

    @pl.when(kj == 0)
    def _():
        m_sc[...] = jnp.full(m_sc.shape, NEG_BIG, F32)
        acc_sc[...] = jnp.zeros(acc_sc.shape, F32)

    def visible_tiles(i, diagonal):
        if not diagonal:
            return nlt, 0
        rc = (i * ATTN_STRIP) // CHUNK
        per_tile = LANES // CHUNK
        return (rc + 1) // per_tile, ((rc + 1) % per_tile) * CHUNK

    def qk(g, u):
        c0 = pl.multiple_of((g * nun + u) * HEAD_PAD, HEAD_PAD)
        s_sc[u] = _dot_nt(q_ref[0, :, pl.ds(c0, HEAD_PAD)], k_ref[0, :, pl.ds(c0, HEAD_PAD)])

    def softmax(g, u, diagonal):
        hd = g * nun + u
        lane = lax.broadcasted_iota(jnp.int32, (ATTN_STRIP, LANES), 1)
        for i in range(nstrip):
            r = slice(i * ATTN_STRIP, (i + 1) * ATTN_STRIP)
            full, part = visible_tiles(i, diagonal)
            mx = None
            for j in range(full + (1 if part else 0)):
                sj = s_sc[u, r, j * LANES:(j + 1) * LANES]
                if j == full:
                    sj = jnp.where(lane < part, sj, NEG_BIG)
                mx = sj if mx is None else jnp.maximum(mx, sj)
            m_prev = m_sc[hd, r, :]
            m_new = jnp.maximum(m_prev, jnp.max(mx, axis=-1, keepdims=True))
            a_sc[u, r, :] = jnp.exp2(m_prev - m_new)
            m_sc[hd, r, :] = m_new
        for i in range(nstrip):
            r = slice(i * ATTN_STRIP, (i + 1) * ATTN_STRIP)
            full, part = visible_tiles(i, diagonal)
            m_new = m_sc[hd, r, :]
            for j in range(nlt):
                c = slice(j * LANES, (j + 1) * LANES)
                if j < full or (j == full and part):
                    p = jnp.exp2(s_sc[u, r, c] - m_new)
                    if j == full:
                        p = jnp.where(lane < part, p, 0.0)
                    p_sc[u, r, c] = p.astype(BF16)
                else:
                    p_sc[u, r, c] = jnp.zeros((ATTN_STRIP, LANES), BF16)

    def pv(g, u):
        hd = g * nun + u
        c0 = pl.multiple_of(hd * V_PAD, V_PAD)
        acc_sc[hd] = acc_sc[hd] * a_sc[u] + _dot(p_sc[u], v_ref[0, :, pl.ds(c0, V_PAD)])

    def group_step(g, diagonal):
        for stage in range(nun + 2):
            if stage < nun:
                qk(g, stage)
            if 0 <= stage - 1 < nun:
                softmax(g, stage - 1, diagonal)
            if 0 <= stage - 2 < nun:
                pv(g, stage - 2)

    @pl.when(kj < qi)
    def _():
        def body(g, c):
            group_step(g, False)
            return c
        lax.fori_loop(0, ngroup, body, 0)

    @pl.when(kj == qi)
    def _():
        left = lax.broadcasted_iota(jnp.int32, (tq, LANES), 1) < MLA_V

        def body(g, c):
            group_step(g, True)
            for u in range(0, nun, 2):
                outs = []
                for hd in (g * nun + u, g * nun + u + 1):
                    acc = acc_sc[hd]
                    outs.append(acc / pltpu.roll(acc, MLA_V, axis=1))
                o0 = pl.multiple_of((g * nun + u) * MLA_V, 2 * MLA_V)
                o_ref[0, :, pl.ds(o0, 2 * MLA_V)] = jnp.where(
                    left, outs[0], pltpu.roll(outs[1], MLA_V, axis=1)).astype(o_ref.dtype)
            return c
        lax.fori_loop(0, ngroup, body, 0)


def _attention(q, k, v, bsz, seq):
    tile = min(ATTN_TILE, seq)
    hps = ATTN_HEADS_PER_STEP
    nun = ATTN_UNROLL_HEADS
    qi_list, kj_list = [], []
    for a in range(seq // tile):
        for b in range(a + 1):
            qi_list.append(a)
            kj_list.append(b)
    qi_tab = jnp.asarray(qi_list, jnp.int32)
    kj_tab = jnp.asarray(kj_list, jnp.int32)
    q3 = q.reshape(bsz, seq, MLA_HEADS * HEAD_PAD)
    k3 = k.reshape(bsz, seq, MLA_HEADS * HEAD_PAD)
    v3 = v.reshape(bsz, seq, MLA_HEADS * V_PAD)
    grid_spec = pltpu.PrefetchScalarGridSpec(
        num_scalar_prefetch=2,
        grid=(bsz, MLA_HEADS // hps, len(qi_list)),
        in_specs=[
            pl.BlockSpec((1, tile, hps * HEAD_PAD), lambda b, hg, t, qt, kt: (b, qt[t], hg)),
            pl.BlockSpec((1, tile, hps * HEAD_PAD), lambda b, hg, t, qt, kt: (b, kt[t], hg)),
            pl.BlockSpec((1, tile, hps * V_PAD), lambda b, hg, t, qt, kt: (b, kt[t], hg)),
        ],
        out_specs=pl.BlockSpec((1, tile, hps * MLA_V), lambda b, hg, t, qt, kt: (b, qt[t], hg)),
        scratch_shapes=[
            pltpu.VMEM((hps, tile, LANES), F32),
            pltpu.VMEM((hps, tile, V_PAD), F32),
            pltpu.VMEM((nun, tile, tile), F32),
            pltpu.VMEM((nun, tile, tile), BF16),
            pltpu.VMEM((nun, tile, LANES), F32),
        ],
    )
    o = pl.pallas_call(
        _attn_kernel,
        grid_spec=grid_spec,
        out_shape=jax.ShapeDtypeStruct((bsz, seq, MLA_HEADS * MLA_V), BF16),
        compiler_params=_cparams(("parallel", "parallel", "arbitrary")),
        name="mla_attention",
    )(qi_tab, kj_tab, q3, k3, v3)
    return o.reshape(bsz * seq, MLA_HEADS * MLA_V)


def _proj_res_kernel(a_ref, w_ref, x_ref, gate_ref, o_ref):
    o_ref[...] = x_ref[...] + gate_ref[0] * _dot(a_ref[...], w_ref[...])


def _proj_residual(a, w, x2, gate, seq):
    t, d = x2.shape
    kdim = a.shape[1]
    tm = min(ROW_TILE, seq)
    tpb = seq // tm
    return pl.pallas_call(
        _proj_res_kernel,
        grid=(t // tm,),
        in_specs=[
            pl.BlockSpec((tm, kdim), lambda i: (i, 0)),
            pl.BlockSpec((kdim, d), lambda i: (0, 0)),
            pl.BlockSpec((tm, d), lambda i: (i, 0)),
            pl.BlockSpec((1, 1, d), lambda i: (i // tpb, 0, 0)),
        ],
        out_specs=pl.BlockSpec((tm, d), lambda i: (i, 0)),
        out_shape=jax.ShapeDtypeStruct((t, d), F32),
        compiler_params=_cparams(("parallel",)),
        name="proj_residual",
    )(a, w, x2, gate)


def _swiglu_chunks(h_sc, wg_ref, wu_ref, wd_ref, acc_sc, lead):
    dff = wg_ref.shape[-1]
    tf = min(FFN_COL_TILE, dff)

    def chunk(c, carry):
        f0 = pl.multiple_of(c * tf, tf)
        h = h_sc[...]
        gt = _dot(h, wg_ref[lead + (slice(None), pl.ds(f0, tf))])
        up = _dot(h, wu_ref[lead + (slice(None), pl.ds(f0, tf))])
        acc_sc[...] += _dot((_silu(gt) * up).astype(BF16), wd_ref[lead + (pl.ds(f0, tf), slice(None))])
        return carry

    acc_sc[...] = jnp.zeros(acc_sc.shape, F32)
    lax.fori_loop(0, dff // tf, chunk, 0, unroll=True)


def _ffn_kernel(x_ref, g_ref, sh_ref, sc_ref, gate_ref, wg_ref, wu_ref, wd_ref, o_ref, h_sc, acc_sc):
    h_sc[...] = _rmsnorm_mod(x_ref[...], g_ref[...], sh_ref[0], sc_ref[0]).astype(BF16)
    _swiglu_chunks(h_sc, wg_ref, wu_ref, wd_ref, acc_sc, ())
    o_ref[...] = x_ref[...] + gate_ref[0] * acc_sc[...]


def _ffn(x2, g, sh, sc, gate, wg, wu, wd, seq):
    t, d = x2.shape
    tm = min(FFN_ROW_TILE, seq)
    tpb = seq // tm
    per_b = lambda i: (i // tpb, 0, 0)
    fixed = lambda i: (0, 0)
    return pl.pallas_call(
        _ffn_kernel,
        grid=(t // tm,),
        in_specs=[
            pl.BlockSpec((tm, d), lambda i: (i, 0)),
            pl.BlockSpec((1, d), fixed),
            pl.BlockSpec((1, 1, d), per_b),
            pl.BlockSpec((1, 1, d), per_b),
            pl.BlockSpec((1, 1, d), per_b),
            pl.BlockSpec(wg.shape, fixed),
            pl.BlockSpec(wu.shape, fixed),
            pl.BlockSpec(wd.shape, fixed),
        ],
        out_specs=pl.BlockSpec((tm, d), lambda i: (i, 0)),
        out_shape=jax.ShapeDtypeStruct((t, d), F32),
        scratch_shapes=[pltpu.VMEM((tm, d), BF16), pltpu.VMEM((tm, d), F32)],
        compiler_params=_cparams(("parallel",)),
        name="swiglu",
    )(x2, g, sh, sc, gate, wg, wu, wd)


INFO_E1, INFO_E2, INFO_W1, INFO_W2, INFO_RANK1, INFO_RANK2 = range(6)


def _router_kernel(x_ref, g_ref, sh_ref, sc_ref, whi_ref, wlo_ref, tri_ref, info_ref, cnt_ref, carry_sc):
    @pl.when(pl.program_id(0) == 0)
    def _():
        carry_sc[...] = jnp.zeros(carry_sc.shape, F32)

    h = _rmsnorm_mod(x_ref[...], g_ref[...], sh_ref[0], sc_ref[0])
    h_hi = h.astype(BF16)
    h_lo = (h - h_hi.astype(F32)).astype(BF16)
    whi = whi_ref[...]
    logits = _dot(h_hi, whi) + _dot(h_lo, whi) + _dot(h_hi, wlo_ref[...])
    lane = lax.broadcasted_iota(jnp.int32, logits.shape, 1)
    neg = jnp.float32(-jnp.inf)
    lg = jnp.where(lane < N_EXPERTS, logits, neg)
    m1 = jnp.max(lg, axis=-1, keepdims=True)
    i1 = jnp.min(jnp.where(lg == m1, lane, LANES), axis=-1, keepdims=True)
    lg2 = jnp.where(lane == i1, neg, lg)
    m2 = jnp.max(lg2, axis=-1, keepdims=True)
    i2 = jnp.min(jnp.where(lg2 == m2, lane, LANES), axis=-1, keepdims=True)
    e2 = jnp.exp(m2 - m1)
    w1 = 1.0 / (1.0 + e2)
    w2 = e2 / (1.0 + e2)
    sel1 = lane == i1
    sel2 = lane == i2
    onehot = jnp.where(sel1 | sel2, 1.0, 0.0)
    before = carry_sc[0:1, :] + _dot(tri_ref[...], onehot.astype(BF16))
    rank1 = jnp.sum(jnp.where(sel1, before, 0.0), axis=-1, keepdims=True)
    rank2 = jnp.sum(jnp.where(sel2, before, 0.0), axis=-1, keepdims=True)
    carry_sc[0:1, :] = carry_sc[0:1, :] + jnp.sum(onehot, axis=0, keepdims=True)
    info = jnp.zeros(logits.shape, F32)
    for k, val in ((INFO_E1, i1.astype(F32)), (INFO_E2, i2.astype(F32)), (INFO_W1, w1), (INFO_W2, w2),
                   (INFO_RANK1, rank1), (INFO_RANK2, rank2)):
        info = jnp.where(lane == k, val, info)
    info_ref[...] = info
    cnt_ref[...] = carry_sc[...]


def _router(x2, g, sh, sc, whi, wlo, seq):
    t, d = x2.shape
    tm = min(ROW_TILE, seq)
    tpb = seq // tm
    per_b = lambda i: (i // tpb, 0, 0)
    r = jnp.arange(tm)
    tri = (r[None, :] < r[:, None]).astype(BF16)
    return pl.pallas_call(
        _router_kernel,
        grid=(t // tm,),
        in_specs=[
            pl.BlockSpec((tm, d), lambda i: (i, 0)),
            pl.BlockSpec((1, d), lambda i: (0, 0)),
            pl.BlockSpec((1, 1, d), per_b),
            pl.BlockSpec((1, 1, d), per_b),
            pl.BlockSpec((d, LANES), lambda i: (0, 0)),
            pl.BlockSpec((d, LANES), lambda i: (0, 0)),
            pl.BlockSpec((tm, tm), lambda i: (0, 0)),
        ],
        out_specs=[
            pl.BlockSpec((tm, LANES), lambda i: (i, 0)),
            pl.BlockSpec((SUBLANES, LANES), lambda i: (0, 0)),
        ],
        out_shape=[
            jax.ShapeDtypeStruct((t, LANES), F32),
            jax.ShapeDtypeStruct((SUBLANES, LANES), F32),
        ],
        scratch_shapes=[pltpu.VMEM((SUBLANES, LANES), F32)],
        compiler_params=_cparams(("arbitrary",)),
        name="moe_router",
    )(x2, g, sh, sc, whi, wlo, tri)


def _row_copy(src_ref, src_row, dst_ref, dst_row, sem):
    return pltpu.make_async_copy(src_ref.at[pl.ds(src_row, 1), :], dst_ref.at[pl.ds(dst_row, 1), :], sem)


def _load_row_indices(r1_ref, r2_ref, idx_sm, isem):
    c1 = pltpu.make_async_copy(r1_ref.at[0], idx_sm.at[0], isem.at[0])
    c2 = pltpu.make_async_copy(r2_ref.at[0], idx_sm.at[1], isem.at[1])
    c1.start()
    c2.start()
    return c1, c2


def _dispatch_kernel(zt_ref, x_ref, g_ref, sh_ref, sc_ref, r1_ref, r2_ref, xs_ref, h_sc, z_sc, idx_sm, sem, isem,
                     zsem):
    tm = x_ref.shape[0]
    ztm = z_sc.shape[0]

    @pl.when(pl.program_id(0) == 0)
    def _():
        z_sc[...] = jnp.zeros(z_sc.shape, F32)
        for i in range(zt_ref.shape[0]):
            cp = pltpu.make_async_copy(z_sc, xs_ref.at[pl.ds(pl.multiple_of(zt_ref[i] * ztm, ztm), ztm), :], zsem)
            cp.start()
            cp.wait()

    step = pl.program_id(0)
    slot = lax.rem(step, 2)
    c1, c2 = _load_row_indices(r1_ref, r2_ref, idx_sm, isem)
    h_sc[slot] = _rmsnorm_mod(x_ref[...], g_ref[...], sh_ref[0], sc_ref[0])
    c1.wait()
    c2.wait()

    def issue(m, c):
        for queue in range(2):
            _row_copy(h_sc.at[slot], m, xs_ref, idx_sm[queue, 0, m], sem.at[slot]).start(priority=queue)
        return c

    def drain_slot(s):
        def drain(m, c):
            for _ in range(2):
                _row_copy(h_sc.at[s], 0, xs_ref, 0, sem.at[s]).wait()
            return c
        lax.fori_loop(0, tm, drain, 0, unroll=DMA_UNROLL)

    lax.fori_loop(0, tm, issue, 0, unroll=DMA_UNROLL)

    @pl.when(step > 0)
    def _():
        drain_slot(1 - slot)

    @pl.when(step == pl.num_programs(0) - 1)
    def _():
        drain_slot(slot)


def _dispatch(x2, g, sh, sc, r1, r2, zero_tiles, rows_pad, seq):
    t, d = x2.shape
    tm = min(DISPATCH_TILE, seq)
    tpb = seq // tm
    per_b = lambda i, zt: (i // tpb, 0, 0)
    idx_spec = pl.BlockSpec((1, 1, tm), lambda i, zt: (i, 0, 0))
    grid_spec = pltpu.PrefetchScalarGridSpec(
        num_scalar_prefetch=1,
        grid=(t // tm,),
        in_specs=[
            pl.BlockSpec((tm, d), lambda i, zt: (i, 0)),
            pl.BlockSpec((1, d), lambda i, zt: (0, 0)),
            pl.BlockSpec((1, 1, d), per_b),
            pl.BlockSpec((1, 1, d), per_b),
            idx_spec,
            idx_spec,
        ],
        out_specs=pl.BlockSpec(memory_space=pl.ANY),
        scratch_shapes=[
            pltpu.VMEM((2, tm, d), F32),
            pltpu.VMEM((MOE_ROW_TILE, d), F32),
            pltpu.SMEM((2, 1, tm), jnp.int32),
            pltpu.SemaphoreType.DMA((2,)),
            pltpu.SemaphoreType.DMA((2,)),
            pltpu.SemaphoreType.DMA,
        ],
    )
    return pl.pallas_call(
        _dispatch_kernel,
        grid_spec=grid_spec,
        out_shape=jax.ShapeDtypeStruct((rows_pad, d), F32),
        compiler_params=_cparams(("arbitrary",)),
        name="moe_dispatch",
    )(zero_tiles, x2, g, sh, sc, r1.reshape(t // tm, 1, tm), r2.reshape(t // tm, 1, tm))


def _expert_ffn_kernel(te_ref, nu_ref, x_ref, wg_ref, wu_ref, wd_ref, o_ref, h_sc, acc_sc):
    del te_ref

    @pl.when(pl.program_id(0) < nu_ref[0])
    def _():
        h_sc[...] = x_ref[...].astype(BF16)
        _swiglu_chunks(h_sc, wg_ref, wu_ref, wd_ref, acc_sc, (0,))
        o_ref[...] = acc_sc[...]

    @pl.when(pl.program_id(0) >= nu_ref[0])
    def _():
        o_ref[...] = jnp.zeros(o_ref.shape, F32)


def _expert_ffn(xs, tile_expert, n_used, wg, wu, wd):
    rows_pad, d = xs.shape
    dff = wg.shape[2]
    tm = MOE_ROW_TILE
    per_expert = lambda i, te, nu: (te[i], 0, 0)
    grid_spec = pltpu.PrefetchScalarGridSpec(
        num_scalar_prefetch=2,
        grid=(rows_pad // tm,),
        in_specs=[
            pl.BlockSpec((tm, d), lambda i, te, nu: (jnp.minimum(i, nu[0] - 1), 0)),
            pl.BlockSpec((1, d, dff), per_expert),
            pl.BlockSpec((1, d, dff), per_expert),
            pl.BlockSpec((1, dff, d), per_expert),
        ],
        out_specs=pl.BlockSpec((tm, d), lambda i, te, nu: (i, 0)),
        scratch_shapes=[pltpu.VMEM((tm, d), BF16), pltpu.VMEM((tm, d), F32)],
    )
    return pl.pallas_call(
        _expert_ffn_kernel,
        grid_spec=grid_spec,
        out_shape=jax.ShapeDtypeStruct((rows_pad, d), F32),
        compiler_params=_cparams(("arbitrary",)),
        name="moe_expert_ffn",
    )(tile_expert, n_used, xs, wg, wu, wd)


def _combine_kernel(x_ref, gate_ref, info_ref, r1_ref, r2_ref, fg_ref, ys_ref, o_ref, y_sc, idx_sm, sem, isem, *,
                    final_norm):
    tm = x_ref.shape[0]
    step = pl.program_id(0)
    n_tiles = pl.num_programs(0) - 1
    slot = lax.rem(step, 2)

    @pl.when(step < n_tiles)
    def _():
        c1, c2 = _load_row_indices(r1_ref, r2_ref, idx_sm, isem)
        c1.wait()
        c2.wait()

        def issue(m, c):
            for queue in range(2):
                _row_copy(ys_ref, idx_sm[queue, 0, m], y_sc.at[slot, queue], m, sem.at[slot]).start(priority=queue)
            return c

        lax.fori_loop(0, tm, issue, 0, unroll=DMA_UNROLL)

    @pl.when(step > 0)
    def _():
        prev = 1 - slot

        def drain(m, c):
            for queue in range(2):
                _row_copy(ys_ref, 0, y_sc.at[prev, queue], 0, sem.at[prev]).wait()
            return c

        lax.fori_loop(0, tm, drain, 0, unroll=DMA_UNROLL)
        info = info_ref[...]
        w1 = info[:, INFO_W1:INFO_W1 + 1]
        w2 = info[:, INFO_W2:INFO_W2 + 1]
        out = x_ref[...] + gate_ref[0] * (w1 * y_sc[prev, 0] + w2 * y_sc[prev, 1])
        if final_norm:
            out = out * lax.rsqrt(jnp.mean(out * out, axis=-1, keepdims=True) + EPS) * fg_ref[...]
        o_ref[...] = out


def _combine(x2, gate, info, r1, r2, ys, final_gain, seq, final_norm):
    t, d = x2.shape
    tm = min(DISPATCH_TILE, seq)
    tpb = seq // tm
    n_tiles = t // tm
    done = lambda i: jnp.maximum(i - 1, 0)
    idx_spec = pl.BlockSpec((1, 1, tm), lambda i: (jnp.minimum(i, n_tiles - 1), 0, 0))
    return pl.pallas_call(
        functools.partial(_combine_kernel, final_norm=final_norm),
        grid=(n_tiles + 1,),
        in_specs=[
            pl.BlockSpec((tm, d), lambda i: (done(i), 0)),
            pl.BlockSpec((1, 1, d), lambda i: (done(i) // tpb, 0, 0)),
            pl.BlockSpec((tm, LANES), lambda i: (done(i), 0)),
            idx_spec,
            idx_spec,
            pl.BlockSpec((1, d), lambda i: (0, 0)),
            pl.BlockSpec(memory_space=pl.ANY),
        ],
        out_specs=pl.BlockSpec((tm, d), lambda i: (done(i), 0)),
        out_shape=jax.ShapeDtypeStruct((t, d), F32),
        scratch_shapes=[
            pltpu.VMEM((2, 2, tm, d), F32),
            pltpu.SMEM((2, 1, tm), jnp.int32),
            pltpu.SemaphoreType.DMA((2,)),
            pltpu.SemaphoreType.DMA((2,)),
        ],
        compiler_params=_cparams(("arbitrary",)),
        name="moe_combine",
    )(x2, gate, info, r1.reshape(n_tiles, 1, tm), r2.reshape(n_tiles, 1, tm), final_gain, ys)


def _moe_plan(info, counts, n_tokens):
    tm = MOE_ROW_TILE
    cnt = counts[0, :N_EXPERTS].astype(jnp.int32)
    padded = ((cnt + tm - 1) // tm) * tm
    ends = jnp.cumsum(padded)
    starts = ends - padded
    e1 = info[:, INFO_E1].astype(jnp.int32)
    e2 = info[:, INFO_E2].astype(jnp.int32)
    r1 = starts[e1] + info[:, INFO_RANK1].astype(jnp.int32)
    r2 = starts[e2] + info[:, INFO_RANK2].astype(jnp.int32)
    n_tiles = (2 * n_tokens) // tm + N_EXPERTS
    n_used = (ends[-1] // tm).reshape(1)
    tile_start = jnp.minimum(jnp.arange(n_tiles, dtype=jnp.int32), n_used - 1) * tm
    tile_expert = jnp.sum(tile_start[:, None] >= ends[None, :], axis=1).astype(jnp.int32)
    group_last = jnp.maximum(ends // tm - 1, 0)
    tail = jnp.minimum(n_used + jnp.arange(N_EXPERTS, dtype=jnp.int32), n_tiles - 1)
    zero_tiles = jnp.concatenate([group_last, tail]).astype(jnp.int32)
    return r1, r2, tile_expert, n_used.astype(jnp.int32), zero_tiles, n_tiles * tm


def _ssm_in_kernel(x_ref, g_ref, sh_ref, sc_ref, wz_ref, wx_ref, wdt_ref, z_out, xbc_out, dt_out):
    h = _rmsnorm_mod(x_ref[...], g_ref[...], sh_ref[0], sc_ref[0]).astype(BF16)
    z_out[...] = _dot(h, wz_ref[...]).astype(BF16)
    xbc_out[...] = _dot(h, wx_ref[...]).astype(BF16)
    dt_out[...] = _dot(h, wdt_ref[...])


def _ssm_in(x2, g, sh, sc, wz, wx, wdt, seq):
    t, d = x2.shape
    tm = min(ROW_TILE, seq)
    tpb = seq // tm
    per_b = lambda i: (i // tpb, 0, 0)
    fixed = lambda i: (0, 0)
    row = lambda i: (i, 0)
    return pl.pallas_call(
        _ssm_in_kernel,
        grid=(t // tm,),
        in_specs=[
            pl.BlockSpec((tm, d), row),
            pl.BlockSpec((1, d), fixed),
            pl.BlockSpec((1, 1, d), per_b),
            pl.BlockSpec((1, 1, d), per_b),
            pl.BlockSpec(wz.shape, fixed),
            pl.BlockSpec(wx.shape, fixed),
            pl.BlockSpec(wdt.shape, fixed),
        ],
        out_specs=[
            pl.BlockSpec((tm, wz.shape[1]), row),
            pl.BlockSpec((tm, wx.shape[1]), row),
            pl.BlockSpec((tm, LANES), row),
        ],
        out_shape=[
            jax.ShapeDtypeStruct((t, wz.shape[1]), BF16),
            jax.ShapeDtypeStruct((t, wx.shape[1]), BF16),
            jax.ShapeDtypeStruct((t, LANES), F32),
        ],
        compiler_params=_cparams(("parallel",)),
        name="ssm_in",
    )(x2, g, sh, sc, wz, wx, wdt)


def _ssd_kernel(z_ref, xbc_ref, dt_ref, cw_ref, cb_ref, dtb_ref, alog_ref, dskip_ref, ng_ref,
                tri_ref, exp_ref, eye_ref, shift_ref, o_ref, ubuf, acs_sc, dtx_sc, state_sc, *, d_inner):
    blk = z_ref.shape[0]
    gw = d_inner // SSM_GROUPS
    gn = SSM_GROUPS * SSM_STATE
    pair = 2 * SSM_HEADDIM

    @pl.when(pl.program_id(1) == 0)
    def _():
        ubuf[0:CONV_TAIL, :] = jnp.zeros((CONV_TAIL, ubuf.shape[1]), ubuf.dtype)
        state_sc[...] = jnp.zeros(state_sc.shape, F32)

    ubuf[CONV_TAIL:CONV_TAIL + blk, :] = xbc_ref[...]

    lane = lax.broadcasted_iota(jnp.int32, (1, LANES), 1)
    heads = d_inner // SSM_HEADDIM
    real = lane < heads
    a = jnp.where(real, -jnp.exp(alog_ref[...]), 0.0)
    v = dt_ref[...] + dtb_ref[...]
    dt = jnp.where(real, jnp.maximum(v, 0.0) + jnp.log(1.0 + jnp.exp(-jnp.abs(v))), 0.0)
    acs = _unpack3(_dot(tri_ref[...], _pack3(dt * a, heads)), heads, real)
    expand = exp_ref[...]
    acs_sc[...] = _dot(_pack3(acs, heads), expand)
    dtx_sc[...] = _dot(_pack3(dt, heads), expand)

    cw = cw_ref[...]
    cbias = cb_ref[...]
    dskip = dskip_ref[...]
    ngain = ng_ref[...]
    eye = eye_ref[...]
    row_i = lax.broadcasted_iota(jnp.int32, (CHUNK, pair), 0)
    col_i = lax.broadcasted_iota(jnp.int32, (CHUNK, pair), 1)
    col_s = jnp.where(col_i >= SSM_HEADDIM, col_i - SSM_HEADDIM, col_i)
    diag = row_i == col_s
    causal = col_s <= row_i
    first_head = col_i < SSM_HEADDIM

    def chunk(c, carry):
        r0 = pl.multiple_of(c * CHUNK, CHUNK)
        taps = _dot(shift_ref[...], ubuf[pl.ds(r0, CHUNK + CONV_TAIL), :])
        u = cbias
        for j in range(SSM_CONV):
            u = u + cw[j:j + 1, :] * taps[j * CHUNK:(j + 1) * CHUNK, :]
        xbc = _silu(u)
        xs = xbc[:, :d_inner]
        bm = xbc[:, d_inner:d_inner + gn].astype(BF16)
        cm = xbc[:, d_inner + gn:].astype(BF16)
        ae = acs_sc[pl.ds(r0, CHUNK), :]
        xdt = xs * dtx_sc[pl.ds(r0, CHUNK), :]
        last = acs_sc[pl.ds(r0 + CHUNK - 1, 1), :]
        from_start = jnp.exp(ae)
        xdte = (xdt * jnp.exp(last - ae)).astype(BF16)
        chunk_decay = jnp.exp(last)
        zc = z_ref[pl.ds(r0, CHUNK), :].astype(F32)
        zgate = _silu(zc)
        for g in range(SSM_GROUPS):
            bg = bm[:, g * SSM_STATE:(g + 1) * SSM_STATE]
            cg = cm[:, g * SSM_STATE:(g + 1) * SSM_STATE]
            gs = slice(g * gw, (g + 1) * gw)
            cbcb = _dot_nt(cg, jnp.concatenate([bg, bg], axis=0))
            s_prev = state_sc[g]
            y_off = _dot(cg, s_prev.astype(BF16)) * from_start[:, gs]
            bg_t = _dot_nt(eye, bg).astype(BF16)
            state_sc[g] = s_prev * chunk_decay[:, gs] + _dot(bg_t, xdte[:, gs])
            ys = []
            for pr in range(gw // pair):
                ps = slice(g * gw + pr * pair, g * gw + (pr + 1) * pair)
                dcol = ae[:, ps]
                drow = jnp.sum(jnp.where(diag, dcol, 0.0), axis=0, keepdims=True)
                decay = jnp.where(causal, jnp.exp(jnp.where(causal, dcol - drow, 0.0)), 0.0)
                m = (cbcb * decay).astype(BF16)
                xp = xdt[:, ps]
                bd = jnp.concatenate([jnp.where(first_head, xp, 0.0), jnp.where(first_head, 0.0, xp)],
                                     axis=0).astype(BF16)
                ys.append(_dot(m, bd) + y_off[:, pr * pair:(pr + 1) * pair])
            yg = (jnp.concatenate(ys, axis=1) + dskip[:, gs] * xs[:, gs]) * zgate[:, gs]
            yn = yg * lax.rsqrt(jnp.mean(yg * yg, axis=-1, keepdims=True) + EPS) * ngain[:, gs]
            o_ref[pl.ds(r0, CHUNK), gs] = yn.astype(o_ref.dtype)
        return carry

    lax.fori_loop(0, blk // CHUNK, chunk, 0, unroll=SSD_UNROLL)
    ubuf[0:CONV_TAIL, :] = ubuf[blk:blk + CONV_TAIL, :]


def _ssd(z, xbc, dt, cw, cb, dtb, alog, dskip, ng, bsz, seq):
    d_inner = z.shape[1]
    cdim = xbc.shape[1]
    blk = min(SSD_BLOCK, seq)
    nblk = seq // blk
    r = jnp.arange(blk)
    tri = (((r[:, None] // CHUNK) == (r[None, :] // CHUNK)) & (r[None, :] <= r[:, None])).astype(BF16)
    heads = d_inner // SSM_HEADDIM
    assert 3 * heads <= LANES
    src = jnp.arange(LANES)[:, None]
    expand = ((src < 3 * heads) & (src % heads == jnp.arange(d_inner)[None, :] // SSM_HEADDIM)).astype(BF16)
    eye = jnp.eye(SSM_STATE, dtype=BF16)
    tap_row = jnp.arange(SSM_CONV * CHUNK)
    src_row = CONV_TAIL + tap_row % CHUNK - (SSM_CONV - 1) + tap_row // CHUNK
    shift = (src_row[:, None] == jnp.arange(CHUNK + CONV_TAIL)[None, :]).astype(BF16)
    row = lambda b, i: (b * nblk + i, 0)
    fixed = lambda b, i: (0, 0)
    return pl.pallas_call(
        functools.partial(_ssd_kernel, d_inner=d_inner),
        grid=(bsz, nblk),
        in_specs=[
            pl.BlockSpec((blk, d_inner), row),
            pl.BlockSpec((blk, cdim), row),
            pl.BlockSpec((blk, LANES), row),
            pl.BlockSpec(cw.shape, fixed),
            pl.BlockSpec(cb.shape, fixed),
            pl.BlockSpec(dtb.shape, fixed),
            pl.BlockSpec(alog.shape, fixed),
            pl.BlockSpec(dskip.shape, fixed),
            pl.BlockSpec(ng.shape, fixed),
            pl.BlockSpec(tri.shape, fixed),
            pl.BlockSpec(expand.shape, fixed),
            pl.BlockSpec(eye.shape, fixed),
            pl.BlockSpec(shift.shape, fixed),
        ],
        out_specs=pl.BlockSpec((blk, d_inner), row),
        out_shape=jax.ShapeDtypeStruct((bsz * seq, d_inner), BF16),
        scratch_shapes=[
            pltpu.VMEM((blk + CONV_TAIL, cdim), BF16),
            pltpu.VMEM((blk, d_inner), F32),
            pltpu.VMEM((blk, d_inner), F32),
            pltpu.VMEM((SSM_GROUPS, SSM_STATE, d_inner // SSM_GROUPS), F32),
        ],
        compiler_params=_cparams(("parallel", "arbitrary")),
        name="ssd",
    )(z, xbc, dt, cw, cb, dtb, alog, dskip, ng, tri, expand, eye, shift)


def _final_norm_kernel(x_ref, g_ref, o_ref):
    x = x_ref[...]
    o_ref[...] = x * lax.rsqrt(jnp.mean(x * x, axis=-1, keepdims=True) + EPS) * g_ref[...]


def _final_norm(x2, g, seq):
    t, d = x2.shape
    tm = min(FFN_ROW_TILE, seq)
    return pl.pallas_call(
        _final_norm_kernel,
        grid=(t // tm,),
        in_specs=[pl.BlockSpec((tm, d), lambda i: (i, 0)), pl.BlockSpec((1, d), lambda i: (0, 0))],
        out_specs=pl.BlockSpec((tm, d), lambda i: (i, 0)),
        out_shape=jax.ShapeDtypeStruct((t, d), F32),
        compiler_params=_cparams(("parallel",)),
        name="final_norm",
    )(x2, g)


def _swap_halves(w):
    half = w.shape[-1] // 2
    return jnp.concatenate([w[..., half:], w[..., :half]], axis=-1)


def _head_block(nope, rope):
    pad = jnp.zeros(rope.shape[:-1] + (HEAD_PAD - MLA_NOPE - MLA_ROPE,), rope.dtype)
    blk = jnp.concatenate([nope, rope, pad], axis=-1)
    return blk.reshape(blk.shape[:-2] + (blk.shape[-2] * HEAD_PAD,))


def _mla_weights(w_in, w_uq, w_ukv):
    d = w_in.shape[0]
    r0 = MLA_Q_LORA + MLA_KV_LORA
    w_kr = w_in[:, r0:]
    z_nope = jnp.zeros((d, 1, MLA_NOPE), w_in.dtype)
    win = jnp.concatenate([w_in[:, :r0], _head_block(z_nope, w_kr[:, None, :]),
                           _head_block(z_nope, _swap_halves(w_kr)[:, None, :])], axis=1)
    uq = w_uq.reshape(MLA_Q_LORA, MLA_HEADS, MLA_NOPE + MLA_ROPE)
    uq_nope, uq_rope = uq[..., :MLA_NOPE], uq[..., MLA_NOPE:]
    wuq = jnp.concatenate([_head_block(uq_nope, uq_rope),
                           _head_block(jnp.zeros_like(uq_nope), _swap_halves(uq_rope))], axis=1)
    ukv = w_ukv.reshape(MLA_KV_LORA, MLA_HEADS, MLA_NOPE + MLA_V)
    uk, uv = ukv[..., :MLA_NOPE], ukv[..., MLA_NOPE:]
    wk = _head_block(uk, jnp.zeros(uk.shape[:-1] + (MLA_ROPE,), uk.dtype))
    wv = jnp.concatenate([uv, jnp.zeros(uv.shape[:-1] + (V_PAD - MLA_V,), uv.dtype)], axis=-1)
    wukv = jnp.concatenate([wk, wv.reshape(MLA_KV_LORA, MLA_HEADS * V_PAD)], axis=1)
    return win.astype(BF16), wuq.astype(BF16), wukv.astype(BF16)


def _rope_tables(positions):
    inv_freq = ROPE_THETA ** (-jnp.arange(0, MLA_ROPE, 2, dtype=F32) / MLA_ROPE)
    ang = positions.astype(F32).reshape(-1)[:, None] * inv_freq
    cos, sin = jnp.cos(ang), jnp.sin(ang)
    t = cos.shape[0]
    lead = jnp.zeros((t, ROPE_OFF), F32)
    tail = jnp.zeros((t, HEAD_PAD - ROPE_OFF - MLA_ROPE), F32)
    scale = (MLA_NOPE + MLA_ROPE) ** -0.5 * LOG2E
    ck =jnp.concatenate([lead, cos, cos, tail], axis=1)
    sk = jnp.concatenate([lead, -sin, sin, tail], axis=1)
    cq = jnp.concatenate([lead + 1.0, cos, cos, tail], axis=1) * scale
    sq = sk * scale
    return cq, sq, ck, sk


def _pad_lanes(v, fill=0.0):
    return jnp.pad(v, [(0, 0)] * (v.ndim - 1) + [(0, LANES - v.shape[-1])], constant_values=fill)


def kernel(x, c, positions, ada_w, ada_b, norm_g, mla_w_in, mla_q_norm, mla_kv_norm, mla_w_uq, mla_w_ukv, mla_w_out, ssm_w_in, ssm_conv_w, ssm_conv_b, ssm_dt_bias, ssm_a_log, ssm_d, ssm_norm, ssm_w_out, ffn_w_gate, ffn_w_up, ffn_w_down, moe_w_router, moe_w_gate, moe_w_up, moe_w_down, final_norm):
    bsz, seq, d = x.shape
    depth = ada_w.shape[0]
    d_inner = ssm_norm.shape[1]
    cdim = ssm_conv_w.shape[2]
    t = bsz * seq
    x2 = x.reshape(t, d)

    c_pad = jnp.pad(c, ((0, SUBLANES - bsz), (0, 0)))
    mod = _adaln(c_pad, ada_w, ada_b)[:, :bsz].reshape(depth, bsz, 6, 1, d)
    cq, sq, ck, sk = _rope_tables(positions)

    for i in range(depth):
        j = i // 2
        sh1, sc1, g1, sh2, sc2, g2 = [mod[i, :, k] for k in range(6)]
        gain1 = norm_g[i, 0][None, :]
        gain2 = norm_g[i, 1][None, :]
        if i % 2 == 0:
            win, wuq, wukv = _mla_weights(mla_w_in[j], mla_w_uq[j], mla_w_ukv[j])
            q, k, v = _mla_in(x2, gain1, sh1, sc1, win, mla_q_norm[j][None, :], mla_kv_norm[j][None, :],
                              wuq, wukv, cq, sq, ck, sk, seq)
            o = _attention(q, k, v, bsz, seq)
            x2 = _proj_residual(o, mla_w_out[j].astype(BF16), x2, g1, seq)
            x2 = _ffn(x2, gain2, sh2, sc2, g2, ffn_w_gate[j].astype(BF16), ffn_w_up[j].astype(BF16),
                      ffn_w_down[j].astype(BF16), seq)
        else:
            w_in = ssm_w_in[j]
            wz = w_in[:, :d_inner].astype(BF16)
            wx = w_in[:, d_inner:d_inner + cdim].astype(BF16)
            wdt = _pad_lanes(w_in[:, d_inner + cdim:]).astype(BF16)
            z, xbc, dt = _ssm_in(x2, gain1, sh1, sc1, wz, wx, wdt, seq)
            yn = _ssd(z, xbc, dt, jnp.pad(ssm_conv_w[j], ((0, SUBLANES - SSM_CONV), (0, 0))),
                      ssm_conv_b[j][None, :], _pad_lanes(ssm_dt_bias[j][None, :]),
                      _pad_lanes(ssm_a_log[j][None, :]), jnp.repeat(ssm_d[j], SSM_HEADDIM)[None, :],
                      ssm_norm[j][None, :], bsz, seq)
            x2 = _proj_residual(yn, ssm_w_out[j].astype(BF16), x2, g1, seq)
            wr = _pad_lanes(moe_w_router[j])
            wr_hi = wr.astype(BF16)
            wr_lo = (wr - wr_hi.astype(F32)).astype(BF16)
            info, counts = _router(x2, gain2, sh2, sc2, wr_hi, wr_lo, seq)
            r1, r2, tile_expert, n_used, zero_tiles, rows_pad = _moe_plan(info, counts, t)
            xs = _dispatch(x2, gain2, sh2, sc2, r1, r2, zero_tiles, rows_pad, seq)
            ys = _expert_ffn(xs, tile_expert, n_used, moe_w_gate[j].astype(BF16), moe_w_up[j].astype(BF16),
                             moe_w_down[j].astype(BF16))
            x2 = _combine(x2, g2, info, r1, r2, ys, final_norm[None, :], seq, i == depth - 1)
    if depth % 2 == 1:
        x2 = _final_norm(x2, final_norm[None, :], seq)
    return x2.reshape(bsz, seq, d)
```

```python
import functools

import jax
import jax.numpy as jnp
from jax import lax
from jax.experimental import pallas as pl
from jax.experimental.pallas import tpu as pltpu

F32 = jnp.float32
BF16 = jnp.bfloat16

EPS = 1e-6
CHUNK = 64
MLA_HEADS = 16
MLA_NOPE = 64
MLA_ROPE = 32
MLA_V = 64
MLA_Q_LORA = 384
MLA_KV_LORA = 256
ROPE_THETA = 10000.0
SSM_HEADDIM = 64
SSM_GROUPS = 4
SSM_STATE = 128
SSM_CONV = 4
N_EXPERTS = 8

LANES = 128
SUBLANES = 8
VMEM_LIMIT = 56 * 1024 * 1024

ROW_TILE = 512
FFN_ROW_TILE = 512
FFN_COL_TILE = 256
ATTN_TILE = 512
ATTN_STRIP = 32
ATTN_HEADS_PER_STEP = 16
ATTN_UNROLL_HEADS = 4
SSD_BLOCK = 512
ADA_COL_TILE = 1536
DISPATCH_TILE = 256
MOE_ROW_TILE = 512
DMA_UNROLL = 8
SSD_UNROLL = 2
CONV_TAIL = 16

HEAD_PAD = LANES
V_PAD = LANES
ROPE_OFF = MLA_NOPE
NEG_BIG = -1e30
LOG2E = 1.4426950408889634


def _cparams(sem):
    return pltpu.CompilerParams(dimension_semantics=sem, vmem_limit_bytes=VMEM_LIMIT)


def _rmsnorm_mod(x, g, sh, sc):
    y = x * lax.rsqrt(jnp.mean(x * x, axis=-1, keepdims=True) + EPS) * g
    return y * (1.0 + sc) + sh


def _silu(v):
    hv = 0.5 * v
    return hv + hv * jnp.tanh(hv)


def _split3(v):
    hi = v.astype(BF16)
    r1 = v - hi.astype(F32)
    mid = r1.astype(BF16)
    lo = (r1 - mid.astype(F32)).astype(BF16)
    return hi, mid, lo


def _pack3(v, width):
    hi, mid, lo = (p.astype(F32) for p in _split3(v))
    return (hi + pltpu.roll(mid, width, axis=1) + pltpu.roll(lo, 2 * width, axis=1)).astype(BF16)


def _unpack3(r, width, real):
    return jnp.where(real, r + pltpu.roll(r, LANES - width, axis=1) + pltpu.roll(r, LANES - 2 * width, axis=1), 0.0)


def _dot(a, b):
    return jnp.dot(a, b, preferred_element_type=F32)


def _dot_nt(a, b):
    return lax.dot_general(a, b, (((1,), (1,)), ((), ())), preferred_element_type=F32)


def _adaln_kernel(c_ref, w_ref, b_ref, o_ref):
    c = c_ref[...]
    cond = _silu(c).astype(BF16)
    o_ref[0] = _dot(cond, w_ref[0].astype(BF16)) + b_ref[0]


def _adaln(c_pad, ada_w, ada_b):
    depth, d, n = ada_w.shape
    bp = c_pad.shape[0]
    tn = min(ADA_COL_TILE, n)
    return pl.pallas_call(
        _adaln_kernel,
        grid=(depth, n // tn),
        in_specs=[
            pl.BlockSpec((bp, d), lambda l, j: (0, 0)),
            pl.BlockSpec((1, d, tn), lambda l, j: (l, 0, j)),
            pl.BlockSpec((1, 1, tn), lambda l, j: (l, 0, j)),
        ],
        out_specs=pl.BlockSpec((1, bp, tn), lambda l, j: (l, 0, j)),
        out_shape=jax.ShapeDtypeStruct((depth, bp, n), F32),
        compiler_params=_cparams(("parallel", "parallel")),
        name="adaln",
    )(c_pad, ada_w, ada_b.reshape(depth, 1, n))


def _mla_in_kernel(x_ref, g_ref, sh_ref, sc_ref, win_ref, qn_ref, kvn_ref, wuq_ref, wukv_ref,
                   cq_ref, sq_ref, ck_ref, sk_ref, q_out, k_out, v_out):
    h = _rmsnorm_mod(x_ref[...], g_ref[...], sh_ref[0], sc_ref[0]).astype(BF16)
    proj = _dot(h, win_ref[...])
    cq = proj[:, :MLA_Q_LORA]
    cq = (cq * lax.rsqrt(jnp.mean(cq * cq, axis=-1, keepdims=True) + EPS) * qn_ref[...]).astype(BF16)
    ckv = proj[:, MLA_Q_LORA:MLA_Q_LORA + MLA_KV_LORA]
    ckv = (ckv * lax.rsqrt(jnp.mean(ckv * ckv, axis=-1, keepdims=True) + EPS) * kvn_ref[...]).astype(BF16)
    r0 = MLA_Q_LORA + MLA_KV_LORA
    kr = proj[:, r0:r0 + HEAD_PAD] * ck_ref[...] + proj[:, r0 + HEAD_PAD:r0 + 2 * HEAD_PAD] * sk_ref[...]
    nq = MLA_HEADS * HEAD_PAD
    qa = _dot(cq, wuq_ref[:, :nq])
    qb = _dot(cq, wuq_ref[:, nq:])
    kv = _dot(ckv, wukv_ref[...])
    cqt = cq_ref[...]
    sqt = sq_ref[...]
    for hd in range(MLA_HEADS):
        sl = slice(hd * HEAD_PAD, (hd + 1) * HEAD_PAD)
        q_out[:, sl] = (qa[:, sl] * cqt + qb[:, sl] * sqt).astype(BF16)
        k_out[:, sl] = (kv[:, sl] + kr).astype(BF16)
    ones_half = jnp.where(lax.broadcasted_iota(jnp.int32, (1, V_PAD), 1) >= MLA_V, 1.0, 0.0)
    for hd in range(MLA_HEADS):
        sl = slice(hd * V_PAD, (hd + 1) * V_PAD)
        v_out[:, sl] = (kv[:, nq + hd * V_PAD:nq + (hd + 1) * V_PAD] + ones_half).astype(BF16)


def _mla_in(x2, g, sh, sc, win, qn, kvn, wuq, wukv, cq, sq, ck, sk, seq):
    t, d = x2.shape
    tm = min(ROW_TILE, seq)
    tpb = seq // tm
    nq = MLA_HEADS * HEAD_PAD
    nv = MLA_HEADS * V_PAD
    row = lambda i: (i, 0)
    fixed = lambda i: (0, 0)
    per_b = lambda i: (i // tpb, 0, 0)
    return pl.pallas_call(
        _mla_in_kernel,
        grid=(t // tm,),
        in_specs=[
            pl.BlockSpec((tm, d), row),
            pl.BlockSpec((1, d), fixed),
            pl.BlockSpec((1, 1, d), per_b),
            pl.BlockSpec((1, 1, d), per_b),
            pl.BlockSpec(win.shape, fixed),
            pl.BlockSpec(qn.shape, fixed),
            pl.BlockSpec(kvn.shape, fixed),
            pl.BlockSpec(wuq.shape, fixed),
            pl.BlockSpec(wukv.shape, fixed),
            pl.BlockSpec((tm, HEAD_PAD), row),
            pl.BlockSpec((tm, HEAD_PAD), row),
            pl.BlockSpec((tm, HEAD_PAD), row),
            pl.BlockSpec((tm, HEAD_PAD), row),
        ],
        out_specs=[
            pl.BlockSpec((tm, nq), row),
            pl.BlockSpec((tm, nq), row),
            pl.BlockSpec((tm, nv), row),
        ],
        out_shape=[
            jax.ShapeDtypeStruct((t, nq), BF16),
            jax.ShapeDtypeStruct((t, nq), BF16),
            jax.ShapeDtypeStruct((t, nv), BF16),
        ],
        compiler_params=_cparams(("parallel",)),
        name="mla_in",
    )(x2, g, sh, sc, win, qn, kvn, wuq, wukv, cq, sq, ck, sk)


def _attn_kernel(qi_tab, kj_tab, q_ref, k_ref, v_ref, o_ref, m_sc, acc_sc, s_sc, p_sc, a_sc):
    t = pl.program_id(2)
    qi = qi_tab[t]
    kj = kj_tab[t]
    tq = q_ref.shape[1]
    tk = k_ref.shape[1]
    nlt = tk // LANES
    nstrip = tq // ATTN_STRIP
    nun = s_sc.shape[0]
    ngroup = q_ref.shape[2] // (nun * HEAD_PAD)

    @pl.when(kj == 0)
    def _():
        m_sc[...] = jnp.full(m_sc.shape, NEG_BIG, F32)
        acc_sc[...] = jnp.zeros(acc_sc.shape, F32)

    def visible_tiles(i, diagonal):
        if not diagonal:
            return nlt, 0
        rc = (i * ATTN_STRIP) // CHUNK
        per_tile = LANES // CHUNK
        return (rc + 1) // per_tile, ((rc + 1) % per_tile) * CHUNK

    def qk(g, u):
        c0 = pl.multiple_of((g * nun + u) * HEAD_PAD, HEAD_PAD)
        s_sc[u] = _dot_nt(q_ref[0, :, pl.ds(c0, HEAD_PAD)], k_ref[0, :, pl.ds(c0, HEAD_PAD)])

    def softmax(g, u, diagonal):
        hd = g * nun + u
        lane = lax.broadcasted_iota(jnp.int32, (ATTN_STRIP, LANES), 1)
        for i in range(nstrip):
            r = slice(i * ATTN_STRIP, (i + 1) * ATTN_STRIP)
            full, part = visible_tiles(i, diagonal)
            mx = None
            for j in range(full + (1 if part else 0)):
                sj = s_sc[u, r, j * LANES:(j + 1) * LANES]
                if j == full:
                    sj = jnp.where(lane < part, sj, NEG_BIG)
                mx = sj if mx is None else jnp.maximum(mx, sj)
            m_prev = m_sc[hd, r, :]
            m_new = jnp.maximum(m_prev, jnp.max(mx, axis=-1, keepdims=True))
            a_sc[u, r, :] = jnp.exp2(m_prev - m_new)
            m_sc[hd, r, :] = m_new
        for i in range(nstrip):
            r = slice(i * ATTN_STRIP, (i + 1) * ATTN_STRIP)
            full, part = visible_tiles(i, diagonal)
            m_new = m_sc[hd, r, :]
            for j in range(nlt):
                c = slice(j * LANES, (j + 1) * LANES)
                if j < full or (j == full and part):
                    p = jnp.exp2(s_sc[u, r, c] - m_new)
                    if j == full:
                        p = jnp.where(lane < part, p, 0.0)
                    p_sc[u, r, c] = p.astype(BF16)
                else:
                    p_sc[u, r, c] = jnp.zeros((ATTN_STRIP, LANES), BF16)

    def pv(g, u):
        hd = g * nun + u
        c0 = pl.multiple_of(hd * V_PAD, V_PAD)
        acc_sc[hd] = acc_sc[hd] * a_sc[u] + _dot(p_sc[u], v_ref[0, :, pl.ds(c0, V_PAD)])

    def group_step(g, diagonal):
        for stage in range(nun + 2):
            if stage < nun:
                qk(g, stage)
            if 0 <= stage - 1 < nun:
                softmax(g, stage - 1, diagonal)
            if 0 <= stage - 2 < nun:
                pv(g, stage - 2)

    @pl.when(kj < qi)
    def _():
        def body(g, c):
            group_step(g, False)
            return c
        lax.fori_loop(0, ngroup, body, 0)

    @pl.when(kj == qi)
    def _():
        left = lax.broadcasted_iota(jnp.int32, (tq, LANES), 1) < MLA_V

        def body(g, c):
            group_step(g, True)
            for u in range(0, nun, 2):
                outs = []
                for hd in (g * nun + u, g * nun + u + 1):
                    acc = acc_sc[hd]
                    outs.append(acc / pltpu.roll(acc, MLA_V, axis=1))
                o0 = pl.multiple_of((g * nun + u) * MLA_V, 2 * MLA_V)
                o_ref[0, :, pl.ds(o0, 2 * MLA_V)] = jnp.where(
                    left, outs[0], pltpu.roll(outs[1], MLA_V, axis=1)).astype(o_ref.dtype)
            return c
        lax.fori_loop(0, ngroup, body, 0)


def _attention(q, k, v, bsz, seq):
    tile = min(ATTN_TILE, seq)
    hps = ATTN_HEADS_PER_STEP
    nun = ATTN_UNROLL_HEADS
    qi_list, kj_list = [], []
    for a in range(seq // tile):
        for b in range(a + 1):
            qi_list.append(a)
            kj_list.append(b)
    qi_tab = jnp.asarray(qi_list, jnp.int32)
    kj_tab = jnp.asarray(kj_list, jnp.int32)
    q3 = q.reshape(bsz, seq, MLA_HEADS * HEAD_PAD)
    k3 = k.reshape(bsz, seq, MLA_HEADS * HEAD_PAD)
    v3 = v.reshape(bsz, seq, MLA_HEADS * V_PAD)
    grid_spec = pltpu.PrefetchScalarGridSpec(
        num_scalar_prefetch=2,
        grid=(bsz, MLA_HEADS // hps, len(qi_list)),
        in_specs=[
            pl.BlockSpec((1, tile, hps * HEAD_PAD), lambda b, hg, t, qt, kt: (b, qt[t], hg)),
            pl.BlockSpec((1, tile, hps * HEAD_PAD), lambda b, hg, t, qt, kt: (b, kt[t], hg)),
            pl.BlockSpec((1, tile, hps * V_PAD), lambda b, hg, t, qt, kt: (b, kt[t], hg)),
        ],
        out_specs=pl.BlockSpec((1, tile, hps * MLA_V), lambda b, hg, t, qt, kt: (b, qt[t], hg)),
        scratch_shapes=[
            pltpu.VMEM((hps, tile, LANES), F32),
            pltpu.VMEM((hps, tile, V_PAD), F32),
            pltpu.VMEM((nun, tile, tile), F32),
            pltpu.VMEM((nun, tile, tile), BF16),
            pltpu.VMEM((nun, tile, LANES), F32),
        ],
    )
    o = pl.pallas_call(
        _attn_kernel,
        grid_spec=grid_spec,
        out_shape=jax.ShapeDtypeStruct((bsz, seq, MLA_HEADS * MLA_V), BF16),
        compiler_params=_cparams(("parallel", "parallel", "arbitrary")),
        name="mla_attention",
    )(qi_tab, kj_tab, q3, k3, v3)
    return o.reshape(bsz * seq, MLA_HEADS * MLA_V)


def _proj_res_kernel(a_ref, w_ref, x_ref, gate_ref, o_ref):
    o_ref[...] = x_ref[...] + gate_ref[0] * _dot(a_ref[...], w_ref[...])


def _proj_residual(a, w, x2, gate, seq):
    t, d = x2.shape
    kdim = a.shape[1]
    tm = min(ROW_TILE, seq)
    tpb = seq // tm
    return pl.pallas_call(
        _proj_res_kernel,
        grid=(t // tm,),
        in_specs=[
            pl.BlockSpec((tm, kdim), lambda i: (i, 0)),
            pl.BlockSpec((kdim, d), lambda i: (0, 0)),
            pl.BlockSpec((tm, d), lambda i: (i, 0)),
            pl.BlockSpec((1, 1, d), lambda i: (i // tpb, 0, 0)),
        ],
        out_specs=pl.BlockSpec((tm, d), lambda i: (i, 0)),
        out_shape=jax.ShapeDtypeStruct((t, d), F32),
        compiler_params=_cparams(("parallel",)),
        name="proj_residual",
    )(a, w, x2, gate)


def _swiglu_chunks(h_sc, wg_ref, wu_ref, wd_ref, acc_sc, lead):
    dff = wg_ref.shape[-1]
    tf = min(FFN_COL_TILE, dff)

    def chunk(c, carry):
        f0 = pl.multiple_of(c * tf, tf)
        h = h_sc[...]
        gt = _dot(h, wg_ref[lead + (slice(None), pl.ds(f0, tf))])
        up = _dot(h, wu_ref[lead + (slice(None), pl.ds(f0, tf))])
        acc_sc[...] += _dot((_silu(gt) * up).astype(BF16), wd_ref[lead + (pl.ds(f0, tf), slice(None))])
        return carry

    acc_sc[...] = jnp.zeros(acc_sc.shape, F32)
    lax.fori_loop(0, dff // tf, chunk, 0, unroll=True)


def _ffn_kernel(x_ref, g_ref, sh_ref, sc_ref, gate_ref, wg_ref, wu_ref, wd_ref, o_ref, h_sc, acc_sc):
    h_sc[...] = _rmsnorm_mod(x_ref[...], g_ref[...], sh_ref[0], sc_ref[0]).astype(BF16)
    _swiglu_chunks(h_sc, wg_ref, wu_ref, wd_ref, acc_sc, ())
    o_ref[...] = x_ref[...] + gate_ref[0] * acc_sc[...]


def _ffn(x2, g, sh, sc, gate, wg, wu, wd, seq):
    t, d = x2.shape
    tm = min(FFN_ROW_TILE, seq)
    tpb = seq // tm
    per_b = lambda i: (i // tpb, 0, 0)
    fixed = lambda i: (0, 0)
    return pl.pallas_call(
        _ffn_kernel,
        grid=(t // tm,),
        in_specs=[
            pl.BlockSpec((tm, d), lambda i: (i, 0)),
            pl.BlockSpec((1, d), fixed),
            pl.BlockSpec((1, 1, d), per_b),
            pl.BlockSpec((1, 1, d), per_b),
            pl.BlockSpec((1, 1, d), per_b),
            pl.BlockSpec(wg.shape, fixed),
            pl.BlockSpec(wu.shape, fixed),
            pl.BlockSpec(wd.shape, fixed),
        ],
        out_specs=pl.BlockSpec((tm, d), lambda i: (i, 0)),
        out_shape=jax.ShapeDtypeStruct((t, d), F32),
        scratch_shapes=[pltpu.VMEM((tm, d), BF16), pltpu.VMEM((tm, d), F32)],
        compiler_params=_cparams(("parallel",)),
        name="swiglu",
    )(x2, g, sh, sc, gate, wg, wu, wd)


INFO_E1, INFO_E2, INFO_W1, INFO_W2, INFO_RANK1, INFO_RANK2 = range(6)


def _router_kernel(x_ref, g_ref, sh_ref, sc_ref, whi_ref, wlo_ref, tri_ref, info_ref, cnt_ref, carry_sc):
    @pl.when(pl.program_id(0) == 0)
    def _():
        carry_sc[...] = jnp.zeros(carry_sc.shape, F32)

    h = _rmsnorm_mod(x_ref[...], g_ref[...], sh_ref[0], sc_ref[0])
    h_hi = h.astype(BF16)
    h_lo = (h - h_hi.astype(F32)).astype(BF16)
    whi = whi_ref[...]
    logits = _dot(h_hi, whi) + _dot(h_lo, whi) + _dot(h_hi, wlo_ref[...])
    lane = lax.broadcasted_iota(jnp.int32, logits.shape, 1)
    neg = jnp.float32(-jnp.inf)
    lg = jnp.where(lane < N_EXPERTS, logits, neg)
    m1 = jnp.max(lg, axis=-1, keepdims=True)
    i1 = jnp.min(jnp.where(lg == m1, lane, LANES), axis=-1, keepdims=True)
    lg2 = jnp.where(lane == i1, neg, lg)
    m2 = jnp.max(lg2, axis=-1, keepdims=True)
    i2 = jnp.min(jnp.where(lg2 == m2, lane, LANES), axis=-1, keepdims=True)
    e2 = jnp.exp(m2 - m1)
    w1 = 1.0 / (1.0 + e2)
    w2 = e2 / (1.0 + e2)
    sel1 = lane == i1
    sel2 = lane == i2
    onehot = jnp.where(sel1 | sel2, 1.0, 0.0)
    before = carry_sc[0:1, :] + _dot(tri_ref[...], onehot.astype(BF16))
    rank1 = jnp.sum(jnp.where(sel1, before, 0.0), axis=-1, keepdims=True)
    rank2 = jnp.sum(jnp.where(sel2, before, 0.0), axis=-1, keepdims=True)
    carry_sc[0:1, :] = carry_sc[0:1, :] + jnp.sum(onehot, axis=0, keepdims=True)
    info = jnp.zeros(logits.shape, F32)
    for k, val in ((INFO_E1, i1.astype(F32)), (INFO_E2, i2.astype(F32)), (INFO_W1, w1), (INFO_W2, w2),
                   (INFO_RANK1, rank1), (INFO_RANK2, rank2)):
        info = jnp.where(lane == k, val, info)
    info_ref[...] = info
    cnt_ref[...] = carry_sc[...]


def _router(x2, g, sh, sc, whi, wlo, seq):
    t, d = x2.shape
    tm = min(ROW_TILE, seq)
    tpb = seq // tm
    per_b = lambda i: (i // tpb, 0, 0)
    r = jnp.arange(tm)
    tri = (r[None, :] < r[:, None]).astype(BF16)
    return pl.pallas_call(
        _router_kernel,
        grid=(t // tm,),
        in_specs=[
            pl.BlockSpec((tm, d), lambda i: (i, 0)),
            pl.BlockSpec((1, d), lambda i: (0, 0)),
            pl.BlockSpec((1, 1, d), per_b),
            pl.BlockSpec((1, 1, d), per_b),
            pl.BlockSpec((d, LANES), lambda i: (0, 0)),
            pl.BlockSpec((d, LANES), lambda i: (0, 0)),
            pl.BlockSpec((tm, tm), lambda i: (0, 0)),
        ],
        out_specs=[
            pl.BlockSpec((tm, LANES), lambda i: (i, 0)),
            pl.BlockSpec((SUBLANES, LANES), lambda i: (0, 0)),
        ],
        out_shape=[
            jax.ShapeDtypeStruct((t, LANES), F32),
            jax.ShapeDtypeStruct((SUBLANES, LANES), F32),
        ],
        scratch_shapes=[pltpu.VMEM((SUBLANES, LANES), F32)],
        compiler_params=_cparams(("arbitrary",)),
        name="moe_router",
    )(x2, g, sh, sc, whi, wlo, tri)


def _row_copy(src_ref, src_row, dst_ref, dst_row, sem):
    src = src_ref.at[pl.ds(pl.multiple_of(src_row * SUBLANES, SUBLANES), SUBLANES), :]
    dst = dst_ref.at[pl.ds(pl.multiple_of(dst_row * SUBLANES, SUBLANES), SUBLANES), :]
    return pltpu.make_async_copy(src, dst, sem)


def _store_token_tiles(ref, value):
    n, d = value.shape
    assert d == SUBLANES * LANES
    for s in range(SUBLANES):
        ref[pl.ds(s, n, stride=SUBLANES), :] = value[:, s * LANES:(s + 1) * LANES]


def _load_token_lanes(ref, n, s):
    return ref[pl.ds(s, n, stride=SUBLANES), :]


def _load_row_indices(r1_ref, r2_ref, idx_sm, isem):
    c1 = pltpu.make_async_copy(r1_ref.at[0], idx_sm.at[0], isem.at[0])
    c2 = pltpu.make_async_copy(r2_ref.at[0], idx_sm.at[1], isem.at[1])
    c1.start()
    c2.start()
    return c1, c2


def _dispatch_kernel(zt_ref, x_ref, g_ref, sh_ref, sc_ref, r1_ref, r2_ref, xs_ref, h_sc, z_sc, idx_sm, sem, isem,
                     zsem):
    tm = x_ref.shape[0]
    ztm = z_sc.shape[0]

    @pl.when(pl.program_id(0) == 0)
    def _():
        z_sc[...] = jnp.zeros(z_sc.shape, F32)
        for i in range(zt_ref.shape[0]):
            cp = pltpu.make_async_copy(z_sc, xs_ref.at[pl.ds(pl.multiple_of(zt_ref[i] * ztm, ztm), ztm), :], zsem)
            cp.start()
            cp.wait()

    step = pl.program_id(0)
    slot = lax.rem(step, 2)
    c1, c2 = _load_row_indices(r1_ref, r2_ref, idx_sm, isem)
    _store_token_tiles(h_sc.at[slot], _rmsnorm_mod(x_ref[...], g_ref[...], sh_ref[0], sc_ref[0]))
    c1.wait()
    c2.wait()

    def issue(m, c):
        for queue in range(2):
            _row_copy(h_sc.at[slot], m, xs_ref, idx_sm[queue, 0, m], sem.at[slot]).start(priority=queue)
        return c

    def drain_slot(s):
        def drain(m, c):
            for _ in range(2):
                _row_copy(h_sc.at[s], 0, xs_ref, 0, sem.at[s]).wait()
            return c
        lax.fori_loop(0, tm, drain, 0, unroll=DMA_UNROLL)

    lax.fori_loop(0, tm, issue, 0, unroll=DMA_UNROLL)

    @pl.when(step > 0)
    def _():
        drain_slot(1 - slot)

    @pl.when(step == pl.num_programs(0) - 1)
    def _():
        drain_slot(slot)


def _dispatch(x2, g, sh, sc, r1, r2, zero_tiles, rows_pad, seq):
    t, d = x2.shape
    tm = min(DISPATCH_TILE, seq)
    tpb = seq // tm
    per_b = lambda i, zt: (i // tpb, 0, 0)
    idx_spec = pl.BlockSpec((1, 1, tm), lambda i, zt: (i, 0, 0))
    grid_spec = pltpu.PrefetchScalarGridSpec(
        num_scalar_prefetch=1,
        grid=(t // tm,),
        in_specs=[
            pl.BlockSpec((tm, d), lambda i, zt: (i, 0)),
            pl.BlockSpec((1, d), lambda i, zt: (0, 0)),
            pl.BlockSpec((1, 1, d), per_b),
            pl.BlockSpec((1, 1, d), per_b),
            idx_spec,
            idx_spec,
        ],
        out_specs=pl.BlockSpec(memory_space=pl.ANY),
        scratch_shapes=[
            pltpu.VMEM((2, tm * SUBLANES, LANES), F32),
            pltpu.VMEM((MOE_ROW_TILE * SUBLANES, LANES), F32),
            pltpu.SMEM((2, 1, tm), jnp.int32),
            pltpu.SemaphoreType.DMA((2,)),
            pltpu.SemaphoreType.DMA((2,)),
            pltpu.SemaphoreType.DMA,
        ],
    )
    return pl.pallas_call(
        _dispatch_kernel,
        grid_spec=grid_spec,
        out_shape=jax.ShapeDtypeStruct((rows_pad * SUBLANES, LANES), F32),
        compiler_params=_cparams(("arbitrary",)),
        name="moe_dispatch",
    )(zero_tiles, x2, g, sh, sc, r1.reshape(t // tm, 1, tm), r2.reshape(t // tm, 1, tm))


def _expert_ffn_kernel(te_ref, nu_ref, x_ref, wg_ref, wu_ref, wd_ref, o_ref, h_sc, acc_sc):
    del te_ref

    @pl.when(pl.program_id(0) < nu_ref[0])
    def _():
        tm = h_sc.shape[0]
        for s in range(SUBLANES):
            h_sc[:, s * LANES:(s + 1) * LANES] = _load_token_lanes(x_ref, tm, s).astype(BF16)
        _swiglu_chunks(h_sc, wg_ref, wu_ref, wd_ref, acc_sc, (0,))
        _store_token_tiles(o_ref, acc_sc[...])

    @pl.when(pl.program_id(0) >= nu_ref[0])
    def _():
        o_ref[...] = jnp.zeros(o_ref.shape, F32)


def _expert_ffn(xs, tile_expert, n_used, wg, wu, wd):
    rows_pad = xs.shape[0] // SUBLANES
    d = wg.shape[1]
    dff = wg.shape[2]
    tm = MOE_ROW_TILE
    per_expert = lambda i, te, nu: (te[i], 0, 0)
    grid_spec = pltpu.PrefetchScalarGridSpec(
        num_scalar_prefetch=2,
        grid=(rows_pad // tm,),
        in_specs=[
            pl.BlockSpec((tm * SUBLANES, LANES), lambda i, te, nu: (jnp.minimum(i, nu[0] - 1), 0)),
            pl.BlockSpec((1, d, dff), per_expert),
            pl.BlockSpec((1, d, dff), per_expert),
            pl.BlockSpec((1, dff, d), per_expert),
        ],
        out_specs=pl.BlockSpec((tm * SUBLANES, LANES), lambda i, te, nu: (i, 0)),
        scratch_shapes=[pltpu.VMEM((tm, d), BF16), pltpu.VMEM((tm, d), F32)],
    )
    return pl.pallas_call(
        _expert_ffn_kernel,
        grid_spec=grid_spec,
        out_shape=jax.ShapeDtypeStruct((rows_pad * SUBLANES, LANES), F32),
        compiler_params=_cparams(("arbitrary",)),
        name="moe_expert_ffn",
    )(tile_expert, n_used, xs, wg, wu, wd)


def _combine_kernel(x_ref, gate_ref, info_ref, r1_ref, r2_ref, fg_ref, ys_ref, o_ref, y_sc, idx_sm, sem, isem, *,
                    final_norm):
    tm = x_ref.shape[0]
    step = pl.program_id(0)
    n_tiles = pl.num_programs(0) - 1
    slot = lax.rem(step, 2)

    @pl.when(step < n_tiles)
    def _():
        c1, c2 = _load_row_indices(r1_ref, r2_ref, idx_sm, isem)
        c1.wait()
        c2.wait()

        def issue(m, c):
            for queue in range(2):
                _row_copy(ys_ref, idx_sm[queue, 0, m], y_sc.at[slot, queue], m, sem.at[slot]).start(priority=queue)
            return c

        lax.fori_loop(0, tm, issue, 0, unroll=DMA_UNROLL)

    @pl.when(step > 0)
    def _():
        prev = 1 - slot

        def drain(m, c):
            for queue in range(2):
                _row_copy(ys_ref, 0, y_sc.at[prev, queue], 0, sem.at[prev]).wait()
            return c

        lax.fori_loop(0, tm, drain, 0, unroll=DMA_UNROLL)
        info = info_ref[...]
        w1 = info[:, INFO_W1:INFO_W1 + 1]
        w2 = info[:, INFO_W2:INFO_W2 + 1]
        mixed = jnp.concatenate(
            [w1 * _load_token_lanes(y_sc.at[prev, 0], tm, s) + w2 * _load_token_lanes(y_sc.at[prev, 1], tm, s)
             for s in range(SUBLANES)], axis=1)
        out = x_ref[...] + gate_ref[0] * mixed
        if final_norm:
            out = out * lax.rsqrt(jnp.mean(out * out, axis=-1, keepdims=True) + EPS) * fg_ref[...]
        o_ref[...] = out


def _combine(x2, gate, info, r1, r2, ys, final_gain, seq, final_norm):
    t, d = x2.shape
    tm = min(DISPATCH_TILE, seq)
    tpb = seq // tm
    n_tiles = t // tm
    done = lambda i: jnp.maximum(i - 1, 0)
    idx_spec = pl.BlockSpec((1, 1, tm), lambda i: (jnp.minimum(i, n_tiles - 1), 0, 0))
    return pl.pallas_call(
        functools.partial(_combine_kernel, final_norm=final_norm),
        grid=(n_tiles + 1,),
        in_specs=[
            pl.BlockSpec((tm, d), lambda i: (done(i), 0)),
            pl.BlockSpec((1, 1, d), lambda i: (done(i) // tpb, 0, 0)),
            pl.BlockSpec((tm, LANES), lambda i: (done(i), 0)),
            idx_spec,
            idx_spec,
            pl.BlockSpec((1, d), lambda i: (0, 0)),
            pl.BlockSpec(memory_space=pl.ANY),
        ],
        out_specs=pl.BlockSpec((tm, d), lambda i: (done(i), 0)),
        out_shape=jax.ShapeDtypeStruct((t, d), F32),
        scratch_shapes=[
            pltpu.VMEM((2, 2, tm * SUBLANES, LANES), F32),
            pltpu.SMEM((2, 1, tm), jnp.int32),
            pltpu.SemaphoreType.DMA((2,)),
            pltpu.SemaphoreType.DMA((2,)),
        ],
        compiler_params=_cparams(("arbitrary",)),
        name="moe_combine",
    )(x2, gate, info, r1.reshape(n_tiles, 1, tm), r2.reshape(n_tiles, 1, tm), final_gain, ys)


def _moe_plan(info, counts, n_tokens):
    tm = MOE_ROW_TILE
    cnt = counts[0, :N_EXPERTS].astype(jnp.int32)
    padded = ((cnt + tm - 1) // tm) * tm
    ends = jnp.cumsum(padded)
    starts = ends - padded
    e1 = info[:, INFO_E1].astype(jnp.int32)
    e2 = info[:, INFO_E2].astype(jnp.int32)
    r1 = starts[e1] + info[:, INFO_RANK1].astype(jnp.int32)
    r2 = starts[e2] + info[:, INFO_RANK2].astype(jnp.int32)
    n_tiles = (2 * n_tokens) // tm + N_EXPERTS
    n_used = (ends[-1] // tm).reshape(1)
    tile_start = jnp.minimum(jnp.arange(n_tiles, dtype=jnp.int32), n_used - 1) * tm
    tile_expert = jnp.sum(tile_start[:, None] >= ends[None, :], axis=1).astype(jnp.int32)
    group_last = jnp.maximum(ends // tm - 1, 0)
    tail = jnp.minimum(n_used + jnp.arange(N_EXPERTS, dtype=jnp.int32), n_tiles - 1)
    zero_tiles = jnp.concatenate([group_last, tail]).astype(jnp.int32)
    return r1, r2, tile_expert, n_used.astype(jnp.int32), zero_tiles, n_tiles * tm


def _ssm_in_kernel(x_ref, g_ref, sh_ref, sc_ref, wz_ref, wx_ref, wdt_ref, z_out, xbc_out, dt_out):
    h = _rmsnorm_mod(x_ref[...], g_ref[...], sh_ref[0], sc_ref[0]).astype(BF16)
    z_out[...] = _dot(h, wz_ref[...]).astype(BF16)
    xbc_out[...] = _dot(h, wx_ref[...]).astype(BF16)
    dt_out[...] = _dot(h, wdt_ref[...])


def _ssm_in(x2, g, sh, sc, wz, wx, wdt, seq):
    t, d = x2.shape
    tm = min(ROW_TILE, seq)
    tpb = seq // tm
    per_b = lambda i: (i // tpb, 0, 0)
    fixed = lambda i: (0, 0)
    row = lambda i: (i, 0)
    return pl.pallas_call(
        _ssm_in_kernel,
        grid=(t // tm,),
        in_specs=[
            pl.BlockSpec((tm, d), row),
            pl.BlockSpec((1, d), fixed),
            pl.BlockSpec((1, 1, d), per_b),
            pl.BlockSpec((1, 1, d), per_b),
            pl.BlockSpec(wz.shape, fixed),
            pl.BlockSpec(wx.shape, fixed),
            pl.BlockSpec(wdt.shape, fixed),
        ],
        out_specs=[
            pl.BlockSpec((tm, wz.shape[1]), row),
            pl.BlockSpec((tm, wx.shape[1]), row),
            pl.BlockSpec((tm, LANES), row),
        ],
        out_shape=[
            jax.ShapeDtypeStruct((t, wz.shape[1]), BF16),
            jax.ShapeDtypeStruct((t, wx.shape[1]), BF16),
            jax.ShapeDtypeStruct((t, LANES), F32),
        ],
        compiler_params=_cparams(("parallel",)),
        name="ssm_in",
    )(x2, g, sh, sc, wz, wx, wdt)


def _ssd_kernel(z_ref, xbc_ref, dt_ref, cw_ref, cb_ref, dtb_ref, alog_ref, dskip_ref, ng_ref,
                tri_ref, exp_ref, eye_ref, shift_ref, o_ref, ubuf, acs_sc, dtx_sc, state_sc, *, d_inner):
    blk = z_ref.shape[0]
    gw = d_inner // SSM_GROUPS
    gn = SSM_GROUPS * SSM_STATE
    pair = 2 * SSM_HEADDIM

    @pl.when(pl.program_id(1) == 0)
    def _():
        ubuf[0:CONV_TAIL, :] = jnp.zeros((CONV_TAIL, ubuf.shape[1]), ubuf.dtype)
        state_sc[...] = jnp.zeros(state_sc.shape, F32)

    ubuf[CONV_TAIL:CONV_TAIL + blk, :] = xbc_ref[...]

    lane = lax.broadcasted_iota(jnp.int32, (1, LANES), 1)
    heads = d_inner // SSM_HEADDIM
    real = lane < heads
    a = jnp.where(real, -jnp.exp(alog_ref[...]), 0.0)
    v = dt_ref[...] + dtb_ref[...]
    dt = jnp.where(real, jnp.maximum(v, 0.0) + jnp.log(1.0 + jnp.exp(-jnp.abs(v))), 0.0)
    acs = _unpack3(_dot(tri_ref[...], _pack3(dt * a, heads)), heads, real)
    expand = exp_ref[...]
    acs_sc[...] = _dot(_pack3(acs, heads), expand)
    dtx_sc[...] = _dot(_pack3(dt, heads), expand)

    cw = cw_ref[...]
    cbias = cb_ref[...]
    dskip = dskip_ref[...]
    ngain = ng_ref[...]
    eye = eye_ref[...]
    row_i = lax.broadcasted_iota(jnp.int32, (CHUNK, pair), 0)
    col_i = lax.broadcasted_iota(jnp.int32, (CHUNK, pair), 1)
    col_s = jnp.where(col_i >= SSM_HEADDIM, col_i - SSM_HEADDIM, col_i)
    diag = row_i == col_s
    causal = col_s <= row_i
    first_head = col_i < SSM_HEADDIM

    def chunk(c, carry):
        r0 = pl.multiple_of(c * CHUNK, CHUNK)
        taps = _dot(shift_ref[...], ubuf[pl.ds(r0, CHUNK + CONV_TAIL), :])
        u = cbias
        for j in range(SSM_CONV):
            u = u + cw[j:j + 1, :] * taps[j * CHUNK:(j + 1) * CHUNK, :]
        xbc = _silu(u)
        xs = xbc[:, :d_inner]
        bm = xbc[:, d_inner:d_inner + gn].astype(BF16)
        cm = xbc[:, d_inner + gn:].astype(BF16)
        ae = acs_sc[pl.ds(r0, CHUNK), :]
        xdt = xs * dtx_sc[pl.ds(r0, CHUNK), :]
        last = acs_sc[pl.ds(r0 + CHUNK - 1, 1), :]
        from_start = jnp.exp(ae)
        xdte = (xdt * jnp.exp(last - ae)).astype(BF16)
        chunk_decay = jnp.exp(last)
        zc = z_ref[pl.ds(r0, CHUNK), :].astype(F32)
        zgate = _silu(zc)
        for g in range(SSM_GROUPS):
            bg = bm[:, g * SSM_STATE:(g + 1) * SSM_STATE]
            cg = cm[:, g * SSM_STATE:(g + 1) * SSM_STATE]
            gs = slice(g * gw, (g + 1) * gw)
            cbcb = _dot_nt(cg, jnp.concatenate([bg, bg], axis=0))
            s_prev = state_sc[g]
            y_off = _dot(cg, s_prev.astype(BF16)) * from_start[:, gs]
            bg_t = _dot_nt(eye, bg).astype(BF16)
            state_sc[g] = s_prev * chunk_decay[:, gs] + _dot(bg_t, xdte[:, gs])
            ys = []
            for pr in range(gw // pair):
                ps = slice(g * gw + pr * pair, g * gw + (pr + 1) * pair)
                dcol = ae[:, ps]
                drow = jnp.sum(jnp.where(diag, dcol, 0.0), axis=0, keepdims=True)
                decay = jnp.where(causal, jnp.exp(jnp.where(causal, dcol - drow, 0.0)), 0.0)
                m = (cbcb * decay).astype(BF16)
                xp = xdt[:, ps]
                bd = jnp.concatenate([jnp.where(first_head, xp, 0.0), jnp.where(first_head, 0.0, xp)],
                                     axis=0).astype(BF16)
                ys.append(_dot(m, bd) + y_off[:, pr * pair:(pr + 1) * pair])
            yg = (jnp.concatenate(ys, axis=1) + dskip[:, gs] * xs[:, gs]) * zgate[:, gs]
            yn = yg * lax.rsqrt(jnp.mean(yg * yg, axis=-1, keepdims=True) + EPS) * ngain[:, gs]
            o_ref[pl.ds(r0, CHUNK), gs] = yn.astype(o_ref.dtype)
        return carry

    lax.fori_loop(0, blk // CHUNK, chunk, 0, unroll=SSD_UNROLL)
    ubuf[0:CONV_TAIL, :] = ubuf[blk:blk + CONV_TAIL, :]


def _ssd(z, xbc, dt, cw, cb, dtb, alog, dskip, ng, bsz, seq):
    d_inner = z.shape[1]
    cdim = xbc.shape[1]
    blk = min(SSD_BLOCK, seq)
    nblk = seq // blk
    r = jnp.arange(blk)
    tri = (((r[:, None] // CHUNK) == (r[None, :] // CHUNK)) & (r[None, :] <= r[:, None])).astype(BF16)
    heads = d_inner // SSM_HEADDIM
    assert 3 * heads <= LANES
    src = jnp.arange(LANES)[:, None]
    expand = ((src < 3 * heads) & (src % heads == jnp.arange(d_inner)[None, :] // SSM_HEADDIM)).astype(BF16)
    eye = jnp.eye(SSM_STATE, dtype=BF16)
    tap_row = jnp.arange(SSM_CONV * CHUNK)
    src_row = CONV_TAIL + tap_row % CHUNK - (SSM_CONV - 1) + tap_row // CHUNK
    shift = (src_row[:, None] == jnp.arange(CHUNK + CONV_TAIL)[None, :]).astype(BF16)
    row = lambda b, i: (b * nblk + i, 0)
    fixed = lambda b, i: (0, 0)
    return pl.pallas_call(
        functools.partial(_ssd_kernel, d_inner=d_inner),
        grid=(bsz, nblk),
        in_specs=[
            pl.BlockSpec((blk, d_inner), row),
            pl.BlockSpec((blk, cdim), row),
            pl.BlockSpec((blk, LANES), row),
            pl.BlockSpec(cw.shape, fixed),
            pl.BlockSpec(cb.shape, fixed),
            pl.BlockSpec(dtb.shape, fixed),
            pl.BlockSpec(alog.shape, fixed),
            pl.BlockSpec(dskip.shape, fixed),
            pl.BlockSpec(ng.shape, fixed),
            pl.BlockSpec(tri.shape, fixed),
            pl.BlockSpec(expand.shape, fixed),
            pl.BlockSpec(eye.shape, fixed),
            pl.BlockSpec(shift.shape, fixed),
        ],
        out_specs=pl.BlockSpec((blk, d_inner), row),
        out_shape=jax.ShapeDtypeStruct((bsz * seq, d_inner), BF16),
        scratch_shapes=[
            pltpu.VMEM((blk + CONV_TAIL, cdim), BF16),
            pltpu.VMEM((blk, d_inner), F32),
            pltpu.VMEM((blk, d_inner), F32),
            pltpu.VMEM((SSM_GROUPS, SSM_STATE, d_inner // SSM_GROUPS), F32),
        ],
        compiler_params=_cparams(("parallel", "arbitrary")),
        name="ssd",
    )(z, xbc, dt, cw, cb, dtb, alog, dskip, ng, tri, expand, eye, shift)


def _final_norm_kernel(x_ref, g_ref, o_ref):
    x = x_ref[...]
    o_ref[...] = x * lax.rsqrt(jnp.mean(x * x, axis=-1, keepdims=True) + EPS) * g_ref[...]


def _final_norm(x2, g, seq):
    t, d = x2.shape
    tm = min(FFN_ROW_TILE, seq)
    return pl.pallas_call(
        _final_norm_kernel,
        grid=(t // tm,),
        in_specs=[pl.BlockSpec((tm, d), lambda i: (i, 0)), pl.BlockSpec((1, d), lambda i: (0, 0))],
        out_specs=pl.BlockSpec((tm, d), lambda i: (i, 0)),
        out_shape=jax.ShapeDtypeStruct((t, d), F32),
        compiler_params=_cparams(("parallel",)),
        name="final_norm",
    )(x2, g)


def _swap_halves(w):
    half = w.shape[-1] // 2
    return jnp.concatenate([w[..., half:], w[..., :half]], axis=-1)


def _head_block(nope, rope):
    pad = jnp.zeros(rope.shape[:-1] + (HEAD_PAD - MLA_NOPE - MLA_ROPE,), rope.dtype)
    blk = jnp.concatenate([nope, rope, pad], axis=-1)
    return blk.reshape(blk.shape[:-2] + (blk.shape[-2] * HEAD_PAD,))


def _mla_weights(w_in, w_uq, w_ukv):
    d = w_in.shape[0]
    r0 = MLA_Q_LORA + MLA_KV_LORA
    w_kr = w_in[:, r0:]
    z_nope = jnp.zeros((d, 1, MLA_NOPE), w_in.dtype)
    win = jnp.concatenate([w_in[:, :r0], _head_block(z_nope, w_kr[:, None, :]),
                           _head_block(z_nope, _swap_halves(w_kr)[:, None, :])], axis=1)
    uq = w_uq.reshape(MLA_Q_LORA, MLA_HEADS, MLA_NOPE + MLA_ROPE)
    uq_nope, uq_rope = uq[..., :MLA_NOPE], uq[..., MLA_NOPE:]
    wuq = jnp.concatenate([_head_block(uq_nope, uq_rope),
                           _head_block(jnp.zeros_like(uq_nope), _swap_halves(uq_rope))], axis=1)
    ukv = w_ukv.reshape(MLA_KV_LORA, MLA_HEADS, MLA_NOPE + MLA_V)
    uk, uv = ukv[..., :MLA_NOPE], ukv[..., MLA_NOPE:]
    wk = _head_block(uk, jnp.zeros(uk.shape[:-1] + (MLA_ROPE,), uk.dtype))
    wv = jnp.concatenate([uv, jnp.zeros(uv.shape[:-1] + (V_PAD - MLA_V,), uv.dtype)], axis=-1)
    wukv = jnp.concatenate([wk, wv.reshape(MLA_KV_LORA, MLA_HEADS * V_PAD)], axis=1)
    return win.astype(BF16), wuq.astype(BF16), wukv.astype(BF16)


def _rope_tables(positions):
    inv_freq = ROPE_THETA ** (-jnp.arange(0, MLA_ROPE, 2, dtype=F32) / MLA_ROPE)
    ang = positions.astype(F32).reshape(-1)[:, None] * inv_freq
    cos, sin = jnp.cos(ang), jnp.sin(ang)
    t = cos.shape[0]
    lead = jnp.zeros((t, ROPE_OFF), F32)
    tail = jnp.zeros((t, HEAD_PAD - ROPE_OFF - MLA_ROPE), F32)
    scale = (MLA_NOPE + MLA_ROPE) ** -0.5 * LOG2E
    ck =jnp.concatenate([lead, cos, cos, tail], axis=1)
    sk = jnp.concatenate([lead, -sin, sin, tail], axis=1)
    cq = jnp.concatenate([lead + 1.0, cos, cos, tail], axis=1) * scale
    sq = sk * scale
    return cq, sq, ck, sk


def _pad_lanes(v, fill=0.0):
    return jnp.pad(v, [(0, 0)] * (v.ndim - 1) + [(0, LANES - v.shape[-1])], constant_values=fill)


def kernel(x, c, positions, ada_w, ada_b, norm_g, mla_w_in, mla_q_norm, mla_kv_norm, mla_w_uq, mla_w_ukv, mla_w_out, ssm_w_in, ssm_conv_w, ssm_conv_b, ssm_dt_bias, ssm_a_log, ssm_d, ssm_norm, ssm_w_out, ffn_w_gate, ffn_w_up, ffn_w_down, moe_w_router, moe_w_gate, moe_w_up, moe_w_down, final_norm):
    bsz, seq, d = x.shape
    depth = ada_w.shape[0]
    d_inner = ssm_norm.shape[1]
    cdim = ssm_conv_w.shape[2]
    t = bsz * seq
    x2 = x.reshape(t, d)

    c_pad = jnp.pad(c, ((0, SUBLANES - bsz), (0, 0)))
    mod = _adaln(c_pad, ada_w, ada_b)[:, :bsz].reshape(depth, bsz, 6, 1, d)
    cq, sq, ck, sk = _rope_tables(positions)

    for i in range(depth):
        j = i // 2
        sh1, sc1, g1, sh2, sc2, g2 = [mod[i, :, k] for k in range(6)]
        gain1 = norm_g[i, 0][None, :]
        gain2 = norm_g[i, 1][None, :]
        if i % 2 == 0:
            win, wuq, wukv = _mla_weights(mla_w_in[j], mla_w_uq[j], mla_w_ukv[j])
            q, k, v = _mla_in(x2, gain1, sh1, sc1, win, mla_q_norm[j][None, :], mla_kv_norm[j][None, :],
                              wuq, wukv, cq, sq, ck, sk, seq)
            o = _attention(q, k, v, bsz, seq)
            x2 = _proj_residual(o, mla_w_out[j].astype(BF16), x2, g1, seq)
            x2 = _ffn(x2, gain2, sh2, sc2, g2, ffn_w_gate[j].astype(BF16), ffn_w_up[j].astype(BF16),
                      ffn_w_down[j].astype(BF16), seq)
        else:
            w_in = ssm_w_in[j]
            wz = w_in[:, :d_inner].astype(BF16)
            wx = w_in[:, d_inner:d_inner + cdim].astype(BF16)
            wdt = _pad_lanes(w_in[:, d_inner + cdim:]).astype(BF16)
            z, xbc, dt = _ssm_in(x2, gain1, sh1, sc1, wz, wx, wdt, seq)
            yn = _ssd(z, xbc, dt, jnp.pad(ssm_conv_w[j], ((0, SUBLANES - SSM_CONV), (0, 0))),
                      ssm_conv_b[j][None, :], _pad_lanes(ssm_dt_bias[j][None, :]),
                      _pad_lanes(ssm_a_log[j][None, :]), jnp.repeat(ssm_d[j], SSM_HEADDIM)[None, :],
                      ssm_norm[j][None, :], bsz, seq)
            x2 = _proj_residual(yn, ssm_w_out[j].astype(BF16), x2, g1, seq)
            wr = _pad_lanes(moe_w_router[j])
            wr_hi = wr.astype(BF16)
            wr_lo = (wr - wr_hi.astype(F32)).astype(BF16)
            info, counts = _router(x2, gain2, sh2, sc2, wr_hi, wr_lo, seq)
            r1, r2, tile_expert, n_used, zero_tiles, rows_pad = _moe_plan(info, counts, t)
            xs = _dispatch(x2, gain2, sh2, sc2, r1, r2, zero_tiles, rows_pad, seq)
            ys = _expert_ffn(xs, tile_expert, n_used, moe_w_gate[j].astype(BF16), moe_w_up[j].astype(BF16),
                             moe_w_down[j].astype(BF16))
            x2 = _combine(x2, g2, info, r1, r2, ys, final_norm[None, :], seq, i == depth - 1)
    if depth % 2 == 1:
        x2 = _final_norm(x2, final_norm[None, :], seq)
    return x2.reshape(bsz, seq, d)
```
